```python
import jax, jax.numpy as jnp
from jax import lax
import numpy as np

D_MODEL = 1024
BATCH = 8
SEQ = 8192
DEPTH = 2

CHUNK = 64
EPS = 1e-6
HG_HEADS = 4
HG_DK = 128
HG_DV = 128
HG_WIDTH = HG_HEADS * HG_DK
HG_VWIDTH = HG_HEADS * HG_DV
DN_HEADS = 4
DN_DK = 128
DN_DV = 128
DN_KW = DN_HEADS * DN_DK
DN_VW = DN_HEADS * DN_DV
CONV_W = 4
D_FF = -(-8 * D_MODEL // (3 * 256)) * 256
IN_SIZES = (HG_WIDTH, HG_WIDTH, HG_VWIDTH, HG_VWIDTH,
            DN_KW, DN_KW, DN_VW,
            DN_VW, DN_HEADS, DN_HEADS,
            D_MODEL, D_MODEL)
IN_WIDTH = sum(IN_SIZES)
IN_SPLITS = tuple(int(s) for s in np.cumsum(IN_SIZES)[:-1])

kernel_name = "hgrn2_gdn_parallel_hybrid"


def rms_norm(x, w):
    xf = x.astype(jnp.float32)
    y = xf * lax.rsqrt(jnp.mean(xf * xf, axis=-1, keepdims=True) + EPS)
    return (y * w.astype(jnp.float32)).astype(x.dtype)


def head_rms_norm(o, w, n_heads, d):
    b, t, _ = o.shape
    oh = o.reshape(b, t, n_heads, d)
    oh = oh * lax.rsqrt(jnp.mean(oh * oh, axis=-1, keepdims=True) + EPS) * w.astype(jnp.float32)
    return oh.reshape(b, t, n_heads * d)


def _chunk(t, n_heads, d):
    b, s, _ = t.shape
    return t.reshape(b, s // CHUNK, CHUNK, n_heads, d).transpose(1, 0, 3, 2, 4)


def _chunk_heads(t):
    b, s, h = t.shape
    return t.reshape(b, s // CHUNK, CHUNK, h).transpose(1, 0, 3, 2)


def _unchunk(o):
    n, b, h, c, d = o.shape
    return o.transpose(1, 0, 3, 2, 4).reshape(b, n * c, h * d)


def causal_short_conv(x, w):
    cc = x.shape[-1]
    return lax.conv_general_dilated(x, w[:, None, :].astype(x.dtype), window_strides=(1,),
                                    padding=[(CONV_W - 1, 0)],
                                    dimension_numbers=('NWC', 'WIO', 'NWC'),
                                    feature_group_count=cc)


def hgrn2_mix(q, f_logit, i, lb):
    q = q.astype(jnp.float32)
    f_logit = f_logit.astype(jnp.float32)
    i = i.astype(jnp.float32)
    log_f = jnp.logaddexp(jnp.log(lb), jnp.log1p(-lb) + jax.nn.log_sigmoid(f_logit))
    k = (1.0 - lb) * jax.nn.sigmoid(-f_logit)
    qc = _chunk(q, HG_HEADS, HG_DK)
    kc = _chunk(k, HG_HEADS, HG_DK)
    vc = _chunk(i, HG_HEADS, HG_DV)
    bc = jnp.cumsum(_chunk(log_f, HG_HEADS, HG_DK), axis=3)
    causal = jnp.tril(jnp.ones((CHUNK, CHUNK), dtype=bool))

    def step(S, xs):
        q_c, k_c, v_c, b_c = xs
        b_last = b_c[:, :, -1:, :]
        diff = b_c[:, :, :, None, :] - b_c[:, :, None, :, :]
        decay = jnp.exp(jnp.where(causal[:, :, None], diff, -jnp.inf))
        attn = jnp.einsum('bhid,bhjd,bhijd->bhij', q_c, k_c, decay)
        o = (jnp.einsum('bhij,bhjv->bhiv', attn, v_c)
             + jnp.einsum('bhid,bhdv->bhiv', q_c * jnp.exp(b_c), S))
        S = (S * jnp.exp(b_last[:, :, 0, :])[..., None]
             + jnp.einsum('bhjd,bhjv->bhdv', k_c * jnp.exp(b_last - b_c), v_c))
        return S, o

    S0 = jnp.zeros((q.shape[0], HG_HEADS, HG_DK, HG_DV), jnp.float32)
    _, o = lax.scan(step, S0, (qc, kc, vc, bc))
    return _unchunk(o)


def gated_deltanet_mix(q, k, v, a, b_logit, a_log, dt_bias):
    q = q.astype(jnp.float32)
    k = k.astype(jnp.float32)
    v = v.astype(jnp.float32)
    b, s, _ = q.shape
    qh = q.reshape(b, s, DN_HEADS, DN_DK)
    kh = k.reshape(b, s, DN_HEADS, DN_DK)
    qh = qh * lax.rsqrt(jnp.sum(qh * qh, -1, keepdims=True) + EPS) * (DN_DK ** -0.5)
    kh = kh * lax.rsqrt(jnp.sum(kh * kh, -1, keepdims=True) + EPS)
    beta = jax.nn.sigmoid(b_logit.astype(jnp.float32))
    g = -jnp.exp(a_log.astype(jnp.float32)) * jax.nn.softplus(
        a.astype(jnp.float32) + dt_bias.astype(jnp.float32))

    qc = _chunk(qh.reshape(b, s, DN_KW), DN_HEADS, DN_DK)
    kc = _chunk(kh.reshape(b, s, DN_KW), DN_HEADS, DN_DK)
    vc = _chunk(v, DN_HEADS, DN_DV)
    betac = _chunk_heads(beta)[..., None]
    gc = jnp.cumsum(_chunk_heads(g), axis=-1)

    causal = jnp.tril(jnp.ones((CHUNK, CHUNK), dtype=bool))
    strict = jnp.tril(jnp.ones((CHUNK, CHUNK), dtype=bool), -1)
    L = jnp.exp(jnp.where(causal, gc[..., :, None] - gc[..., None, :], -jnp.inf))
    k_beta = kc * betac
    v_beta = vc * betac
    A = jnp.where(strict, jnp.einsum('nbhid,nbhjd->nbhij', k_beta, kc) * L, 0.0)
    eye = jnp.eye(CHUNK, dtype=jnp.float32)
    M = eye + A
    T = lax.linalg.triangular_solve(M, jnp.broadcast_to(eye, M.shape), left_side=True,
                                    lower=True, unit_diagonal=True)
    u = jnp.einsum('nbhij,nbhjv->nbhiv', T, v_beta)
    w = jnp.einsum('nbhij,nbhjd->nbhid', T, k_beta * jnp.exp(gc)[..., None])
    attn_qk = jnp.einsum('nbhid,nbhjd->nbhij', qc, kc) * L
    qg = qc * jnp.exp(gc)[..., None]
    kd = kc * jnp.exp(gc[..., -1:] - gc)[..., None]
    dl = jnp.exp(gc[..., -1])

    def step(S, xs):
        u_c, w_c, a_c, qg_c, kd_c, dl_c = xs
        v_new = u_c - jnp.einsum('bhcd,bhdv->bhcv', w_c, S)
        o = jnp.einsum('bhcd,bhdv->bhcv', qg_c, S) + jnp.einsum('bhij,bhjv->bhiv', a_c, v_new)
        S = S * dl_c[..., None, None] + jnp.einsum('bhcd,bhcv->bhdv', kd_c, v_new)
        return S, o

    S0 = jnp.zeros((b, DN_HEADS, DN_DK, DN_DV), jnp.float32)
    _, o = lax.scan(step, S0, (u, w, attn_qk, qg, kd, dl))
    return _unchunk(o)


def hybrid_layer(x, lb, pre_mix_w, w_in, hg_norm_w, conv_w, dn_a_log, dn_dt_bias, dn_norm_w,
                 w_o_hg, w_o_dn, w_out, post_mix_w, pre_ffn_w, w_gate_up, w_down, post_ffn_w):
    u = rms_norm(x, pre_mix_w)
    proj = jnp.einsum('btd,de->bte', u, w_in)
    (hg_q, hg_f, hg_i, hg_g, dn_q, dn_k, dn_v, dn_g, dn_a, dn_b,
     m_a, m_b) = jnp.split(proj, IN_SPLITS, axis=-1)

    o_a = hgrn2_mix(hg_q, hg_f, hg_i, lb)
    o_a = head_rms_norm(o_a, hg_norm_w, HG_HEADS, HG_DV) * jax.nn.silu(hg_g.astype(jnp.float32))

    qkv = jax.nn.silu(causal_short_conv(jnp.concatenate([dn_q, dn_k, dn_v], axis=-1), conv_w))
    cq, ck, cv = jnp.split(qkv, (DN_KW, 2 * DN_KW), axis=-1)
    o_b = gated_deltanet_mix(cq, ck, cv, dn_a, dn_b, dn_a_log, dn_dt_bias)
    o_b = head_rms_norm(o_b, dn_norm_w, DN_HEADS, DN_DV) * jax.nn.silu(dn_g.astype(jnp.float32))

    y_a = jnp.einsum('bte,ed->btd', o_a.astype(x.dtype), w_o_hg)
    y_b = jnp.einsum('bte,ed->btd', o_b.astype(x.dtype), w_o_dn)
    merged = jax.nn.sigmoid(m_a) * y_a + jax.nn.sigmoid(m_b) * y_b
    mix_out = jnp.einsum('btd,de->bte', merged, w_out)
    h = x + rms_norm(mix_out, post_mix_w)

    v = rms_norm(h, pre_ffn_w)
    gate, up = jnp.split(jnp.einsum('btd,df->btf', v, w_gate_up), 2, axis=-1)
    f = jnp.einsum('btf,fd->btd', jax.nn.silu(gate) * up, w_down)
    return h + rms_norm(f, post_ffn_w)


def setup_inputs(seed: int = 0) -> dict:
    key = jax.random.key(seed)
    ks = jax.random.split(key, 20)
    f32 = jnp.float32

    def nrm(k, shape, scale):
        return jax.random.normal(k, shape, f32) * scale

    def gain(k, n):
        return 1.0 + 0.02 * jax.random.normal(k, (DEPTH, n), f32)

    dt = jnp.exp(jax.random.uniform(ks[6], (DEPTH, DN_HEADS), f32, np.log(1e-3), np.log(1e-1)))
    return {
        "x": jax.random.normal(ks[0], (BATCH, SEQ, D_MODEL), f32),
        "hg_lower_bounds": nrm(ks[1], (DEPTH, HG_WIDTH), 0.1),
        "pre_mix_w": gain(ks[2], D_MODEL),
        "w_in": nrm(ks[3], (DEPTH, D_MODEL, IN_WIDTH), D_MODEL ** -0.5),
        "hg_norm_w": gain(ks[4], HG_DV),
        "conv_w": nrm(ks[5], (DEPTH, CONV_W, 2 * DN_KW + DN_VW), CONV_W ** -0.5),
        "dn_a_log": jnp.log(jax.random.uniform(ks[7], (DEPTH, DN_HEADS), f32, 1.0, 16.0)),
        "dn_dt_bias": dt + jnp.log(-jnp.expm1(-dt)),
        "dn_norm_w": gain(ks[8], DN_DV),
        "w_o_hg": nrm(ks[9], (DEPTH, HG_VWIDTH, D_MODEL), HG_VWIDTH ** -0.5),
        "w_o_dn": nrm(ks[10], (DEPTH, DN_VW, D_MODEL), DN_VW ** -0.5),
        "w_out": nrm(ks[11], (DEPTH, D_MODEL, D_MODEL), D_MODEL ** -0.5),
        "post_mix_w": gain(ks[12], D_MODEL),
        "pre_ffn_w": gain(ks[13], D_MODEL),
        "w_gate_up": nrm(ks[14], (DEPTH, D_MODEL, 2 * D_FF), D_MODEL ** -0.5),
        "w_down": nrm(ks[15], (DEPTH, D_FF, D_MODEL), D_FF ** -0.5),
        "post_ffn_w": gain(ks[16], D_MODEL),
    }


def reference(x, hg_lower_bounds, pre_mix_w, w_in, hg_norm_w, conv_w, dn_a_log, dn_dt_bias,
              dn_norm_w, w_o_hg, w_o_dn, w_out, post_mix_w, pre_ffn_w, w_gate_up, w_down,
              post_ffn_w):
    lbs = jnp.cumsum(jax.nn.softmax(hg_lower_bounds.astype(jnp.float32), axis=0), axis=0)
    lbs = lbs - lbs[0]
    h = x
    for l in range(DEPTH):
        h = hybrid_layer(h, lbs[l], pre_mix_w[l], w_in[l], hg_norm_w[l], conv_w[l], dn_a_log[l],
                         dn_dt_bias[l], dn_norm_w[l], w_o_hg[l], w_o_dn[l], w_out[l],
                         post_mix_w[l], pre_ffn_w[l], w_gate_up[l], w_down[l], post_ffn_w[l])
    return h
```

```python
import functools

import jax
import jax.numpy as jnp
from jax import lax
from jax.experimental import pallas as pl
from jax.experimental.pallas import tpu as pltpu

F32 = jnp.float32
BF16 = jnp.bfloat16

EPS = 1e-6
CHUNK = 64
HEAD_DIM = 128
N_HEADS = 4
HEADS_W = N_HEADS * HEAD_DIM
CONV_W = 4
SUBLANES = 8
CONV_HALO = SUBLANES
VMEM_LIMIT = 56 * 1024 * 1024


def _dot(a, b):
    return jnp.dot(a, b, preferred_element_type=F32)


def _dot_nt(a, b):
    return lax.dot_general(a, b, (((1,), (1,)), ((), ())), preferred_element_type=F32)


def _dot_tn(a, b):
    return lax.dot_general(a, b, (((0,), (0,)), ((), ())), preferred_element_type=F32)


def _sigmoid(x):
    return 1.0 / (1.0 + jnp.exp(-x))


def _silu(x):
    return x * _sigmoid(x)


def _rms(x, w):
    return x * lax.rsqrt(jnp.mean(x * x, axis=-1, keepdims=True) + EPS) * w


def _tril_ones(n, dtype):
    r = lax.broadcasted_iota(jnp.int32, (n, n), 0)
    c = lax.broadcasted_iota(jnp.int32, (n, n), 1)
    return jnp.where(c <= r, 1.0, 0.0).astype(dtype)


def _chunk_cumsum(x, tril):
    hi = x.astype(BF16)
    lo = (x - hi.astype(F32)).astype(BF16)
    return _dot(tril, hi) + _dot(tril, lo)


def _group_row_bcast(x, group, row):
    n, w = x.shape
    x3 = x.reshape(n // group, group, w)
    return jnp.broadcast_to(x3[:, row:row + 1, :], x3.shape).reshape(n, w)


def _inproj_kernel(x_ref, pw_ref, w_ref, wab_ref, lbraw_ref,
                   hq_ref, hk_ref, hb_ref, hi_ref, hg_ref, dqkv_ref, dg_ref, ab_ref, m_ref,
                   *, layer, tm):
    x = x_ref[...]
    u = _rms(x, pw_ref[...]).astype(BF16)

    def proj(c0, c1):
        return _dot(u, w_ref[:, c0:c1])

    W = HEADS_W
    hq_ref[...] = proj(0, W).astype(BF16)
    f = proj(W, 2 * W)
    hi_ref[...] = proj(2 * W, 3 * W).astype(BF16)
    hg_ref[...] = proj(3 * W, 4 * W).astype(BF16)
    dqkv_ref[...] = proj(4 * W, 7 * W).astype(BF16)
    dg_ref[...] = proj(7 * W, 8 * W).astype(BF16)
    m_ref[...] = proj(8 * W, 8 * W + m_ref.shape[1]).astype(BF16)
    ab_ref[...] = _dot(u, wab_ref[...])

    depth = lbraw_ref.shape[0]
    rows = [lbraw_ref[r:r + 1, :] for r in range(depth)]
    mx = functools.reduce(jnp.maximum, rows)
    es = [jnp.exp(r - mx) for r in rows]
    tot = functools.reduce(lambda a, b: a + b, es)
    cum = functools.reduce(lambda a, b: a + b, es[:layer + 1])
    lb = (cum - es[0]) / tot

    e = jnp.exp(-jnp.abs(f))
    log_sig = jnp.minimum(f, 0.0) - jnp.log1p(e)
    a = jnp.log(lb)
    c = jnp.log1p(-lb) + log_sig
    log_f = jnp.maximum(a, c) + jnp.log1p(jnp.exp(-jnp.abs(a - c)))
    inv = 1.0 / (1.0 + e)
    hk_ref[...] = ((1.0 - lb) * jnp.where(f >= 0.0, e * inv, inv)).astype(BF16)

    tril = _tril_ones(CHUNK, BF16)
    for c0 in range(0, tm, CHUNK):
        hb_ref[c0:c0 + CHUNK, :] = _chunk_cumsum(log_f[c0:c0 + CHUNK, :], tril)


def _inproj(x, pre_w, w_main, w_ab, lb_raw, layer, tm):
    n, d = x.shape
    wm = w_main.shape[1]
    m_w = wm - 8 * HEADS_W
    row = lambda i: (i, 0)
    const = lambda i: (0, 0)
    resident = dict(pipeline_mode=pl.Buffered(1))
    out_shapes = [
        jax.ShapeDtypeStruct((n, HEADS_W), BF16),
        jax.ShapeDtypeStruct((n, HEADS_W), BF16),
        jax.ShapeDtypeStruct((n, HEADS_W), F32),
        jax.ShapeDtypeStruct((n, HEADS_W), BF16),
        jax.ShapeDtypeStruct((n, HEADS_W), BF16),
        jax.ShapeDtypeStruct((n, 3 * HEADS_W), BF16),
        jax.ShapeDtypeStruct((n, HEADS_W), BF16),
        jax.ShapeDtypeStruct((n, 128), F32),
        jax.ShapeDtypeStruct((n, m_w), BF16),
    ]
    out_specs = [pl.BlockSpec((tm, s.shape[1]), row) for s in out_shapes]
    return pl.pallas_call(
        functools.partial(_inproj_kernel, layer=layer, tm=tm),
        grid=(n // tm,),
        in_specs=[
            pl.BlockSpec((tm, d), row),
            pl.BlockSpec((1, d), const),
            pl.BlockSpec((d, wm), const, **resident),
            pl.BlockSpec((d, 128), const, **resident),
            pl.BlockSpec(lb_raw.shape, const),
        ],
        out_specs=out_specs,
        out_shape=out_shapes,
        compiler_params=pltpu.CompilerParams(dimension_semantics=("arbitrary",),
                                             vmem_limit_bytes=VMEM_LIMIT),
        name=f"inproj_l{layer}",
    )(x, pre_w, w_main, w_ab, lb_raw)


def _head_norm_gate(o, w, g):
    return _rms(o, w) * _silu(g)


def _hgrn_kernel(q_ref, k_ref, b_ref, v_ref, g_ref, nw_ref, o_ref, st_ref, oacc_ref, *, tt):
    C = CHUNK

    @pl.when(pl.program_id(1) == 0)
    def _():
        st_ref[...] = jnp.zeros_like(st_ref)

    row = lax.broadcasted_iota(jnp.int32, (C, C), 0)
    col = lax.broadcasted_iota(jnp.int32, (C, C), 1)
    levels = (32, 16, 8)
    level_masks = [((row // m) % 2 == 1) & ((col // m) == (row // m) - 1) for m in levels]
    diag_masks = [(col == (row // SUBLANES) * SUBLANES + jj) & (jj <= row % SUBLANES)
                  for jj in range(SUBLANES)]
    ones = jnp.ones((HEAD_DIM, C), BF16)

    def chunk(ci, carry):
        r0 = pl.multiple_of(ci * C, C)
        for h in range(N_HEADS):
            cs = slice(h * HEAD_DIM, (h + 1) * HEAD_DIM)
            q = q_ref[pl.ds(r0, C), cs].astype(F32)
            k = k_ref[pl.ds(r0, C), cs].astype(F32)
            v = v_ref[pl.ds(r0, C), cs]
            b = b_ref[pl.ds(r0, C), cs]
            st = st_ref[h]

            attn = jnp.zeros((C, C), F32)
            for m, mask in zip(levels, level_masks):
                be = _group_row_bcast(b, 2 * m, m - 1)
                x = jnp.exp(-jnp.abs(b - be))
                s = _dot_nt((q * x).astype(BF16), (k * x).astype(BF16))
                attn = attn + jnp.where(mask, s, 0.0)
            ps = []
            for jj in range(SUBLANES):
                bj = _group_row_bcast(b, SUBLANES, jj)
                kj = _group_row_bcast(k, SUBLANES, jj)
                ps.append((q * kj * jnp.exp(jnp.minimum(b - bj, 0.0))).astype(BF16))
            rs = _dot(jnp.concatenate(ps, axis=0), ones)
            for jj in range(SUBLANES):
                attn = attn + jnp.where(diag_masks[jj], rs[jj * C:(jj + 1) * C, :], 0.0)

            o = _dot(attn.astype(BF16), v) + _dot_nt((q * jnp.exp(b)).astype(BF16), st.astype(BF16))
            b_last = b[C - 1:C, :]
            kd = (k * jnp.exp(b_last - b)).astype(BF16)
            st_ref[h] = st * jnp.exp(b_last) + _dot_tn(v, kd)
            oacc_ref[pl.ds(r0, C), cs] = o
        return carry

    lax.fori_loop(0, tt // C, chunk, 0)

    for h in range(N_HEADS):
        cs = slice(h * HEAD_DIM, (h + 1) * HEAD_DIM)
        o_ref[:, cs] = _head_norm_gate(oacc_ref[:, cs], nw_ref[...],
                                       g_ref[:, cs].astype(F32)).astype(BF16)


def _hgrn(q, k, b, v, g, norm_w, batch, seq, tt, layer):
    n = batch * seq
    nt = seq // tt
    row = lambda bi, ti: (bi * nt + ti, 0)
    const = lambda bi, ti: (0, 0)
    spec = pl.BlockSpec((tt, HEADS_W), row)
    return pl.pallas_call(
        functools.partial(_hgrn_kernel, tt=tt),
        grid=(batch, nt),
        in_specs=[spec, spec, spec, spec, spec, pl.BlockSpec((1, HEAD_DIM), const)],
        out_specs=spec,
        out_shape=jax.ShapeDtypeStruct((n, HEADS_W), BF16),
        scratch_shapes=[pltpu.VMEM((N_HEADS, HEAD_DIM, HEAD_DIM), F32),
                        pltpu.VMEM((tt, HEADS_W), F32)],
        compiler_params=pltpu.CompilerParams(dimension_semantics=("arbitrary", "arbitrary"),
                                             vmem_limit_bytes=VMEM_LIMIT),
        name=f"hgrn_l{layer}",
    )(q, k, b, v, g, norm_w)


def _gdn_kernel(qkv_ref, cw_ref, ab_ref, par_ref, g_ref, nw_ref, o_ref,
                s_ref, halo_ref, ext_ref, q_s, k_s, v_s, beta_s, gc_s, oacc_ref, *, tt):
    C = CHUNK
    W = HEADS_W

    @pl.when(pl.program_id(1) == 0)
    def _():
        s_ref[...] = jnp.zeros_like(s_ref)
        halo_ref[...] = jnp.zeros_like(halo_ref)

    ext_ref[0:CONV_HALO, :] = halo_ref[...]
    ext_ref[CONV_HALO:, :] = qkv_ref[...].astype(F32)
    halo_ref[...] = ext_ref[tt:tt + CONV_HALO, :]
    acc = None
    for wi in range(CONV_W):
        off = CONV_HALO - (CONV_W - 1) + wi
        term = ext_ref[off:off + tt, :] * cw_ref[wi:wi + 1, :]
        acc = term if acc is None else acc + term
    qkv = _silu(acc)

    for h in range(N_HEADS):
        cs = slice(h * HEAD_DIM, (h + 1) * HEAD_DIM)
        qh = qkv[:, h * HEAD_DIM:(h + 1) * HEAD_DIM]
        kh = qkv[:, W + h * HEAD_DIM:W + (h + 1) * HEAD_DIM]
        q_s[:, cs] = qh * lax.rsqrt(jnp.sum(qh * qh, axis=-1, keepdims=True) + EPS) * (HEAD_DIM ** -0.5)
        k_s[:, cs] = kh * lax.rsqrt(jnp.sum(kh * kh, axis=-1, keepdims=True) + EPS)
    v_s[...] = qkv[:, 2 * W:3 * W]

    ab = ab_ref[...]
    a_log = par_ref[0:1, :]
    dt_bias = par_ref[1:2, :]
    z = ab + dt_bias
    softplus = jnp.maximum(z, 0.0) + jnp.log1p(jnp.exp(-jnp.abs(z)))
    g_all = -jnp.exp(a_log) * softplus
    beta_all = _sigmoid(ab)
    tril = _tril_ones(C, BF16)
    for h in range(N_HEADS):
        cs = slice(h * HEAD_DIM, (h + 1) * HEAD_DIM)
        beta_s[:, cs] = jnp.broadcast_to(beta_all[:, N_HEADS + h:N_HEADS + h + 1], (tt, HEAD_DIM))
    for c0 in range(0, tt, C):
        gc = _chunk_cumsum(g_all[c0:c0 + C, :], tril)
        for h in range(N_HEADS):
            cs = slice(h * HEAD_DIM, (h + 1) * HEAD_DIM)
            gc_s[c0:c0 + C, cs] = jnp.broadcast_to(gc[:, h:h + 1], (C, HEAD_DIM))

    row = lax.broadcasted_iota(jnp.int32, (C, C), 0)
    col = lax.broadcasted_iota(jnp.int32, (C, C), 1)
    causal = col <= row
    strict = col < row
    eye = jnp.where(col == row, 1.0, 0.0).astype(F32)

    def chunk(ci, carry):
        r0 = pl.multiple_of(ci * C, C)
        for h in range(N_HEADS):
            cs = slice(h * HEAD_DIM, (h + 1) * HEAD_DIM)
            q = q_s[pl.ds(r0, C), cs]
            k = k_s[pl.ds(r0, C), cs]
            v = v_s[pl.ds(r0, C), cs]
            beta = beta_s[pl.ds(r0, C), cs]
            gcb = gc_s[pl.ds(r0, C), cs]
            s = s_ref[h]

            eg = jnp.exp(gcb)
            kb = k * beta
            g_last = gcb[C - 1:C, :]
            g_row = jnp.transpose(gcb)[:C, :]
            decay = jnp.exp(jnp.minimum(gcb[:, :C] - g_row, 0.0))
            kbf = k.astype(BF16)
            qk = _dot_nt(jnp.concatenate([q, kb], axis=0).astype(BF16), kbf)
            attn = jnp.where(causal, qk[:C] * decay, 0.0)
            a = jnp.where(strict, qk[C:] * decay, 0.0)

            t = eye - a
            p = a
            for _ in range(5):
                pb = p.astype(BF16)
                p = _dot(pb, pb)
                t = t + _dot(t.astype(BF16), p.astype(BF16))
            uw = _dot(t.astype(BF16), jnp.concatenate([v * beta, kb * eg], axis=1).astype(BF16))
            ws = _dot(jnp.concatenate([uw[:, HEAD_DIM:], q * eg], axis=0).astype(BF16), s.astype(BF16))
            v_new = uw[:, :HEAD_DIM] - ws[:C]
            vnb = v_new.astype(BF16)
            oacc_ref[pl.ds(r0, C), cs] = ws[C:] + _dot(attn.astype(BF16), vnb)
            kd = (k * jnp.exp(g_last - gcb)).astype(BF16)
            s_ref[h] = s * jnp.exp(g_last) + _dot_tn(kd, vnb)
        return carry

    lax.fori_loop(0, tt // C, chunk, 0)

    for h in range(N_HEADS):
        cs = slice(h * HEAD_DIM, (h + 1) * HEAD_DIM)
        o_ref[:, cs] = _head_norm_gate(oacc_ref[:, cs], nw_ref[...],
                                       g_ref[:, cs].astype(F32)).astype(BF16)


def _gdn(qkv, conv_w, ab, par, g, norm_w, batch, seq, tt, layer):
    n = batch * seq
    nt = seq // tt
    row = lambda bi, ti: (bi * nt + ti, 0)
    const = lambda bi, ti: (0, 0)
    return pl.pallas_call(
        functools.partial(_gdn_kernel, tt=tt),
        grid=(batch, nt),
        in_specs=[pl.BlockSpec((tt, 3 * HEADS_W), row),
                  pl.BlockSpec((CONV_W, 3 * HEADS_W), const),
                  pl.BlockSpec((tt, 128), row),
                  pl.BlockSpec((2, 128), const),
                  pl.BlockSpec((tt, HEADS_W), row),
                  pl.BlockSpec((1, HEAD_DIM), const)],
        out_specs=pl.BlockSpec((tt, HEADS_W), row),
        out_shape=jax.ShapeDtypeStruct((n, HEADS_W), BF16),
        scratch_shapes=[pltpu.VMEM((N_HEADS, HEAD_DIM, HEAD_DIM), F32),
                        pltpu.VMEM((CONV_HALO, 3 * HEADS_W), F32),
                        pltpu.VMEM((tt + CONV_HALO, 3 * HEADS_W), F32),
                        pltpu.VMEM((tt, HEADS_W), F32),
                        pltpu.VMEM((tt, HEADS_W), F32),
                        pltpu.VMEM((tt, HEADS_W), F32),
                        pltpu.VMEM((tt, HEADS_W), F32),
                        pltpu.VMEM((tt, HEADS_W), F32),
                        pltpu.VMEM((tt, HEADS_W), F32)],
        compiler_params=pltpu.CompilerParams(dimension_semantics=("arbitrary", "arbitrary"),
                                             vmem_limit_bytes=VMEM_LIMIT),
        name=f"gdn_l{layer}",
    )(qkv, conv_w, ab, par, g, norm_w)


def _post_kernel(x_ref, oa_ref, ob_ref, m_ref, woa_ref, wob_ref, wout_ref, pmw_ref, pfw_ref,
                 wg_ref, wu_ref, wd_ref, pow_ref, out_ref, *, ff_chunk):
    d = x_ref.shape[1]
    ya = _dot(oa_ref[...], woa_ref[...])
    yb = _dot(ob_ref[...], wob_ref[...])
    merged = (_sigmoid(m_ref[:, :d].astype(F32)) * ya + _sigmoid(m_ref[:, d:].astype(F32)) * yb)
    mix = _dot(merged.astype(BF16), wout_ref[...])
    h = x_ref[...] + _rms(mix, pmw_ref[...])
    v = _rms(h, pfw_ref[...]).astype(BF16)
    d_ff = wg_ref.shape[1]
    f = None
    for c0 in range(0, d_ff, ff_chunk):
        gate = _dot(v, wg_ref[:, c0:c0 + ff_chunk])
        up = _dot(v, wu_ref[:, c0:c0 + ff_chunk])
        part = _dot((_silu(gate) * up).astype(BF16), wd_ref[c0:c0 + ff_chunk, :])
        f = part if f is None else f + part
    out_ref[...] = h + _rms(f, pow_ref[...])


def _post(x, oa, ob, m, woa, wob, wout, pmw, pfw, wg, wu, wd, pow_, tm, layer):
    n, d = x.shape
    d_ff = wg.shape[1]
    ff_chunk = d_ff // 2 if (d_ff // 2) % 128 == 0 else d_ff
    row = lambda i: (i, 0)
    const = lambda i: (0, 0)
    resident = dict(pipeline_mode=pl.Buffered(1))

    def wspec(w):
        return pl.BlockSpec(w.shape, const, **resident)

    return pl.pallas_call(
        functools.partial(_post_kernel, ff_chunk=ff_chunk),
        grid=(n // tm,),
        in_specs=[pl.BlockSpec((tm, d), row),
                  pl.BlockSpec((tm, oa.shape[1]), row),
                  pl.BlockSpec((tm, ob.shape[1]), row),
                  pl.BlockSpec((tm, m.shape[1]), row),
                  wspec(woa), wspec(wob), wspec(wout),
                  pl.BlockSpec((1, d), const), pl.BlockSpec((1, d), const),
                  wspec(wg), wspec(wu), wspec(wd),
                  pl.BlockSpec((1, d), const)],
        out_specs=pl.BlockSpec((tm, d), row),
        out_shape=jax.ShapeDtypeStruct((n, d), F32),
        compiler_params=pltpu.CompilerParams(dimension_semantics=("arbitrary",),
                                             vmem_limit_bytes=VMEM_LIMIT),
        name=f"post_l{layer}",
    )(x, oa, ob, m, woa, wob, wout, pmw, pfw, wg, wu, wd, pow_)


def _pick_tile(n, want):
    t = min(n, want)
    while n % t:
        t //= 2
    return t


def kernel(x, hg_lower_bounds, pre_mix_w, w_in, hg_norm_w, conv_w, dn_a_log, dn_dt_bias, dn_norm_w,
           w_o_hg, w_o_dn, w_out, post_mix_w, pre_ffn_w, w_gate_up, w_down, post_ffn_w):
    batch, seq, d = x.shape
    depth = w_in.shape[0]
    n = batch * seq
    W = HEADS_W
    d_ff = w_down.shape[1]
    assert seq % CHUNK == 0 and w_in.shape[2] == 8 * W + 2 * N_HEADS + 2 * d
    tm = _pick_tile(n, 512)
    tt = _pick_tile(seq, 512)

    h = x.reshape(n, d)
    lb_raw = hg_lower_bounds.astype(F32)
    for l in range(depth):
        wl = w_in[l]
        w_main = jnp.concatenate([wl[:, :8 * W], wl[:, 8 * W + 2 * N_HEADS:]], axis=1).astype(BF16)
        w_ab = jnp.pad(wl[:, 8 * W:8 * W + 2 * N_HEADS], ((0, 0), (0, 128 - 2 * N_HEADS))).astype(BF16)
        hq, hk, hb, hi, hg, dqkv, dg, ab, m = _inproj(
            h, pre_mix_w[l].reshape(1, d), w_main, w_ab, lb_raw, l, tm)

        oa = _hgrn(hq, hk, hb, hi, hg, hg_norm_w[l].reshape(1, HEAD_DIM), batch, seq, tt, l)

        par = jnp.pad(jnp.stack([dn_a_log[l], dn_dt_bias[l]]).astype(F32),
                      ((0, 0), (0, 128 - N_HEADS)))
        ob = _gdn(dqkv, conv_w[l].astype(F32), ab, par, dg, dn_norm_w[l].reshape(1, HEAD_DIM),
                  batch, seq, tt, l)

        wgu = w_gate_up[l]
        h = _post(h, oa, ob, m, w_o_hg[l].astype(BF16), w_o_dn[l].astype(BF16), w_out[l].astype(BF16),
                  post_mix_w[l].reshape(1, d), pre_ffn_w[l].reshape(1, d),
                  wgu[:, :d_ff].astype(BF16), wgu[:, d_ff:].astype(BF16), w_down[l].astype(BF16),
                  post_ffn_w[l].reshape(1, d), tm, l)
    return h.reshape(batch, seq, d)
```

```python
import functools

import jax
import jax.numpy as jnp
from jax import lax
from jax.experimental import pallas as pl
from jax.experimental.pallas import tpu as pltpu

F32 = jnp.float32
BF16 = jnp.bfloat16

EPS = 1e-6
CHUNK = 64
HEAD_DIM = 128
N_HEADS = 4
HEADS_W = N_HEADS * HEAD_DIM
CONV_W = 4
SUBLANES = 8
CONV_HALO = SUBLANES
STREAM_GROUP = 16
VMEM_LIMIT = 56 * 1024 * 1024


def _dot(a, b):
    return jnp.dot(a, b, preferred_element_type=F32)


def _dot_nt(a, b):
    return lax.dot_general(a, b, (((1,), (1,)), ((), ())), preferred_element_type=F32)


def _dot_tn(a, b):
    return lax.dot_general(a, b, (((0,), (0,)), ((), ())), preferred_element_type=F32)


def _sigmoid(x):
    return 1.0 / (1.0 + jnp.exp(-x))


def _silu(x):
    return x * _sigmoid(x)


def _rms(x, w):
    return x * lax.rsqrt(jnp.mean(x * x, axis=-1, keepdims=True) + EPS) * w


def _tril_ones(n, dtype):
    r = lax.broadcasted_iota(jnp.int32, (n, n), 0)
    c = lax.broadcasted_iota(jnp.int32, (n, n), 1)
    return jnp.where(c <= r, 1.0, 0.0).astype(dtype)


def _chunk_cumsum(x, tril):
    hi = x.astype(BF16)
    lo = (x - hi.astype(F32)).astype(BF16)
    return _dot(tril, hi) + _dot(tril, lo)


def _group_row_bcast(x, group, row):
    n, w = x.shape
    x3 = x.reshape(n // group, group, w)
    return jnp.broadcast_to(x3[:, row:row + 1, :], x3.shape).reshape(n, w)


def _inproj_kernel(x_ref, pw_ref, w_ref, wab_ref, lbraw_ref,
                   hq_ref, hk_ref, hb_ref, hi_ref, hg_ref, dqkv_ref, dg_ref, ab_ref, m_ref,
                   *, layer, tm):
    x = x_ref[...]
    u = _rms(x, pw_ref[...]).astype(BF16)

    def proj(c0, c1):
        return _dot(u, w_ref[:, c0:c1])

    W = HEADS_W
    hq_ref[...] = proj(0, W).astype(BF16)
    f = proj(W, 2 * W)
    hi_ref[...] = proj(2 * W, 3 * W).astype(BF16)
    hg_ref[...] = proj(3 * W, 4 * W).astype(BF16)
    dqkv_ref[...] = proj(4 * W, 7 * W).astype(BF16)
    dg_ref[...] = proj(7 * W, 8 * W).astype(BF16)
    m_ref[...] = proj(8 * W, 8 * W + m_ref.shape[1]).astype(BF16)
    ab_ref[...] = _dot(u, wab_ref[...])

    depth = lbraw_ref.shape[0]
    rows = [lbraw_ref[r:r + 1, :] for r in range(depth)]
    mx = functools.reduce(jnp.maximum, rows)
    es = [jnp.exp(r - mx) for r in rows]
    tot = functools.reduce(lambda a, b: a + b, es)
    cum = functools.reduce(lambda a, b: a + b, es[:layer + 1])
    lb = (cum - es[0]) / tot

    e = jnp.exp(-jnp.abs(f))
    log_sig = jnp.minimum(f, 0.0) - jnp.log1p(e)
    a = jnp.log(lb)
    c = jnp.log1p(-lb) + log_sig
    log_f = jnp.maximum(a, c) + jnp.log1p(jnp.exp(-jnp.abs(a - c)))
    inv = 1.0 / (1.0 + e)
    hk_ref[...] = ((1.0 - lb) * jnp.where(f >= 0.0, e * inv, inv)).astype(BF16)

    tril = _tril_ones(CHUNK, BF16)
    for c0 in range(0, tm, CHUNK):
        hb_ref[c0:c0 + CHUNK, :] = _chunk_cumsum(log_f[c0:c0 + CHUNK, :], tril)


def _inproj(x, pre_w, w_main, w_ab, lb_raw, layer, tm):
    n, d = x.shape
    wm = w_main.shape[1]
    m_w = wm - 8 * HEADS_W
    row = lambda i: (i, 0)
    const = lambda i: (0, 0)
    resident = dict(pipeline_mode=pl.Buffered(1))
    out_shapes = [
        jax.ShapeDtypeStruct((n, HEADS_W), BF16),
        jax.ShapeDtypeStruct((n, HEADS_W), BF16),
        jax.ShapeDtypeStruct((n, HEADS_W), F32),
        jax.ShapeDtypeStruct((n, HEADS_W), BF16),
        jax.ShapeDtypeStruct((n, HEADS_W), BF16),
        jax.ShapeDtypeStruct((n, 3 * HEADS_W), BF16),
        jax.ShapeDtypeStruct((n, HEADS_W), BF16),
        jax.ShapeDtypeStruct((n, 128), F32),
        jax.ShapeDtypeStruct((n, m_w), BF16),
    ]
    out_specs = [pl.BlockSpec((tm, s.shape[1]), row) for s in out_shapes]
    return pl.pallas_call(
        functools.partial(_inproj_kernel, layer=layer, tm=tm),
        grid=(n // tm,),
        in_specs=[
            pl.BlockSpec((tm, d), row),
            pl.BlockSpec((1, d), const),
            pl.BlockSpec((d, wm), const, **resident),
            pl.BlockSpec((d, 128), const, **resident),
            pl.BlockSpec(lb_raw.shape, const),
        ],
        out_specs=out_specs,
        out_shape=out_shapes,
        compiler_params=pltpu.CompilerParams(dimension_semantics=("arbitrary",),
                                             vmem_limit_bytes=VMEM_LIMIT),
        name=f"inproj_l{layer}",
    )(x, pre_w, w_main, w_ab, lb_raw)


def _head_norm_gate(o, w, g):
    return _rms(o, w) * _silu(g)


def _head_cols(h):
    return slice(h * HEAD_DIM, (h + 1) * HEAD_DIM)


def _hgrn_kernel(q_ref, k_ref, b_ref, v_ref, g_ref, nw_ref, o_ref, st_ref, *, nb, group):
    C = CHUNK

    @pl.when(pl.program_id(0) == 0)
    def _():
        st_ref[...] = jnp.zeros_like(st_ref)

    row = lax.broadcasted_iota(jnp.int32, (C, C), 0)
    col = lax.broadcasted_iota(jnp.int32, (C, C), 1)
    levels = (32, 16, 8)
    level_masks = [((row // m) % 2 == 1) & ((col // m) == (row // m) - 1) for m in levels]
    diag_masks = [(col == (row // SUBLANES) * SUBLANES + jj) & (jj <= row % SUBLANES)
                  for jj in range(SUBLANES)]
    ones = jnp.ones((HEAD_DIM, C), BF16)
    nw = nw_ref[...]

    streams = [(bi, h) for bi in range(nb) for h in range(N_HEADS)]
    for g0 in range(0, len(streams), group):
        grp = streams[g0:g0 + group]
        n = len(grp)
        q = [q_ref[bi, :, _head_cols(h)].astype(F32) for bi, h in grp]
        k = [k_ref[bi, :, _head_cols(h)].astype(F32) for bi, h in grp]
        v = [v_ref[bi, :, _head_cols(h)] for bi, h in grp]
        b = [b_ref[bi, :, _head_cols(h)] for bi, h in grp]
        st = [st_ref[bi * N_HEADS + h] for bi, h in grp]

        attn = [jnp.zeros((C, C), F32) for _ in range(n)]
        for m, mask in zip(levels, level_masks):
            for i in range(n):
                be = _group_row_bcast(b[i], 2 * m, m - 1)
                x = jnp.exp(-jnp.abs(b[i] - be))
                s = _dot_nt((q[i] * x).astype(BF16), (k[i] * x).astype(BF16))
                attn[i] = attn[i] + jnp.where(mask, s, 0.0)
        for i in range(n):
            ps = []
            for jj in range(SUBLANES):
                bj = _group_row_bcast(b[i], SUBLANES, jj)
                kj = _group_row_bcast(k[i], SUBLANES, jj)
                ps.append((q[i] * kj * jnp.exp(jnp.minimum(b[i] - bj, 0.0))).astype(BF16))
            rs = _dot(jnp.concatenate(ps, axis=0), ones)
            for jj in range(SUBLANES):
                attn[i] = attn[i] + jnp.where(diag_masks[jj], rs[jj * C:(jj + 1) * C, :], 0.0)

        o = [_dot_nt((q[i] * jnp.exp(b[i])).astype(BF16), st[i].astype(BF16)) for i in range(n)]
        o = [o[i] + _dot(attn[i].astype(BF16), v[i]) for i in range(n)]
        for i, (bi, h) in enumerate(grp):
            b_last = b[i][C - 1:C, :]
            kd = (k[i] * jnp.exp(b_last - b[i])).astype(BF16)
            st_ref[bi * N_HEADS + h] = st[i] * jnp.exp(b_last) + _dot_tn(v[i], kd)
        for i, (bi, h) in enumerate(grp):
            cs = _head_cols(h)
            o_ref[bi, :, cs] = _head_norm_gate(o[i], nw, g_ref[bi, :, cs].astype(F32)).astype(BF16)


def _hgrn(q, k, b, v, g, norm_w, batch, seq, layer):
    blk = lambda t: (0, t, 0)
    const = lambda t: (0, 0)
    spec = pl.BlockSpec((batch, CHUNK, HEADS_W), blk)
    r3 = lambda a: a.reshape(batch, seq, a.shape[-1])
    out = pl.pallas_call(
        functools.partial(_hgrn_kernel, nb=batch, group=STREAM_GROUP),
        grid=(seq // CHUNK,),
        in_specs=[spec, spec, spec, spec, spec, pl.BlockSpec((1, HEAD_DIM), const)],
        out_specs=spec,
        out_shape=jax.ShapeDtypeStruct((batch, seq, HEADS_W), BF16),
        scratch_shapes=[pltpu.VMEM((batch * N_HEADS, HEAD_DIM, HEAD_DIM), F32)],
        compiler_params=pltpu.CompilerParams(dimension_semantics=("arbitrary",),
                                             vmem_limit_bytes=VMEM_LIMIT),
        name=f"hgrn_l{layer}",
    )(r3(q), r3(k), r3(b), r3(v), r3(g), norm_w)
    return out.reshape(batch * seq, HEADS_W)


def _gdn_kernel(qkv_ref, cw_ref, ab_ref, par_ref, g_ref, nw_ref, o_ref, s_ref, ext_ref, *, nb, group):
    C = CHUNK
    W = HEADS_W

    @pl.when(pl.program_id(0) == 0)
    def _():
        s_ref[...] = jnp.zeros_like(s_ref)
        ext_ref[:, 0:CONV_HALO, :] = jnp.zeros((nb, CONV_HALO, 3 * W), F32)

    ext_ref[:, CONV_HALO:, :] = qkv_ref[...].astype(F32)
    acc = None
    for wi in range(CONV_W):
        off = CONV_HALO - (CONV_W - 1) + wi
        term = ext_ref[:, off:off + C, :] * cw_ref[wi:wi + 1, :]
        acc = term if acc is None else acc + term
    ext_ref[:, 0:CONV_HALO, :] = ext_ref[:, C:C + CONV_HALO, :]
    qkv = _silu(acc)

    a_log = par_ref[0:1, :]
    dt_bias = par_ref[1:2, :]
    tril = _tril_ones(C, BF16)
    beta_all, gc_all = [], []
    for bi in range(nb):
        ab = ab_ref[bi]
        z = ab + dt_bias
        softplus = jnp.maximum(z, 0.0) + jnp.log1p(jnp.exp(-jnp.abs(z)))
        gc_all.append(_chunk_cumsum(-jnp.exp(a_log) * softplus, tril))
        beta_all.append(_sigmoid(ab))

    row = lax.broadcasted_iota(jnp.int32, (C, C), 0)
    col = lax.broadcasted_iota(jnp.int32, (C, C), 1)
    causal = col <= row
    strict = col < row
    eye = jnp.where(col == row, 1.0, 0.0).astype(F32)
    nw = nw_ref[...]

    streams = [(bi, h) for bi in range(nb) for h in range(N_HEADS)]
    for g0 in range(0, len(streams), group):
        grp = streams[g0:g0 + group]
        n = len(grp)
        q, k, v, beta, gcb, s = [], [], [], [], [], []
        for bi, h in grp:
            qh = qkv[bi, :, h * HEAD_DIM:(h + 1) * HEAD_DIM]
            kh = qkv[bi, :, W + h * HEAD_DIM:W + (h + 1) * HEAD_DIM]
            q.append(qh * lax.rsqrt(jnp.sum(qh * qh, axis=-1, keepdims=True) + EPS) * (HEAD_DIM ** -0.5))
            k.append(kh * lax.rsqrt(jnp.sum(kh * kh, axis=-1, keepdims=True) + EPS))
            v.append(qkv[bi, :, 2 * W + h * HEAD_DIM:2 * W + (h + 1) * HEAD_DIM])
            beta.append(jnp.broadcast_to(beta_all[bi][:, N_HEADS + h:N_HEADS + h + 1], (C, HEAD_DIM)))
            gcb.append(jnp.broadcast_to(gc_all[bi][:, h:h + 1], (C, HEAD_DIM)))
            s.append(s_ref[bi * N_HEADS + h])

        eg = [jnp.exp(gcb[i]) for i in range(n)]
        kb = [k[i] * beta[i] for i in range(n)]
        decay = []
        for i in range(n):
            g_row = jnp.transpose(gcb[i])[:C, :]
            decay.append(jnp.exp(jnp.minimum(gcb[i][:, :C] - g_row, 0.0)))
        qk = [_dot_nt(jnp.concatenate([q[i], kb[i]], axis=0).astype(BF16), k[i].astype(BF16))
              for i in range(n)]
        attn = [jnp.where(causal, qk[i][:C] * decay[i], 0.0) for i in range(n)]
        a = [jnp.where(strict, qk[i][C:] * decay[i], 0.0) for i in range(n)]

        t = [eye - a[i] for i in range(n)]
        p = a
        for _ in range(5):
            pb = [p[i].astype(BF16) for i in range(n)]
            p = [_dot(pb[i], pb[i]) for i in range(n)]
            t = [t[i] + _dot(t[i].astype(BF16), p[i].astype(BF16)) for i in range(n)]
        uw = [_dot(t[i].astype(BF16),
                   jnp.concatenate([v[i] * beta[i], kb[i] * eg[i]], axis=1).astype(BF16)) for i in range(n)]
        ws = [_dot(jnp.concatenate([uw[i][:, HEAD_DIM:], q[i] * eg[i]], axis=0).astype(BF16),
                   s[i].astype(BF16)) for i in range(n)]
        vnb = [(uw[i][:, :HEAD_DIM] - ws[i][:C]).astype(BF16) for i in range(n)]
        o = [ws[i][C:] + _dot(attn[i].astype(BF16), vnb[i]) for i in range(n)]
        for i, (bi, h) in enumerate(grp):
            g_last = gcb[i][C - 1:C, :]
            kd = (k[i] * jnp.exp(g_last - gcb[i])).astype(BF16)
            s_ref[bi * N_HEADS + h] = s[i] * jnp.exp(g_last) + _dot_tn(kd, vnb[i])
        for i, (bi, h) in enumerate(grp):
            cs = _head_cols(h)
            o_ref[bi, :, cs] = _head_norm_gate(o[i], nw, g_ref[bi, :, cs].astype(F32)).astype(BF16)


def _gdn(qkv, conv_w, ab, par, g, norm_w, batch, seq, layer):
    blk = lambda t: (0, t, 0)
    const = lambda t: (0, 0)
    r3 = lambda a: a.reshape(batch, seq, a.shape[-1])
    out = pl.pallas_call(
        functools.partial(_gdn_kernel, nb=batch, group=STREAM_GROUP),
        grid=(seq // CHUNK,),
        in_specs=[pl.BlockSpec((batch, CHUNK, 3 * HEADS_W), blk),
                  pl.BlockSpec((CONV_W, 3 * HEADS_W), const),
                  pl.BlockSpec((batch, CHUNK, 128), blk),
                  pl.BlockSpec((2, 128), const),
                  pl.BlockSpec((batch, CHUNK, HEADS_W), blk),
                  pl.BlockSpec((1, HEAD_DIM), const)],
        out_specs=pl.BlockSpec((batch, CHUNK, HEADS_W), blk),
        out_shape=jax.ShapeDtypeStruct((batch, seq, HEADS_W), BF16),
        scratch_shapes=[pltpu.VMEM((batch * N_HEADS, HEAD_DIM, HEAD_DIM), F32),
                        pltpu.VMEM((batch, CHUNK + CONV_HALO, 3 * HEADS_W), F32)],
        compiler_params=pltpu.CompilerParams(dimension_semantics=("arbitrary",),
                                             vmem_limit_bytes=VMEM_LIMIT),
        name=f"gdn_l{layer}",
    )(r3(qkv), conv_w, r3(ab), par, r3(g), norm_w)
    return out.reshape(batch * seq, HEADS_W)


def _post_kernel(x_ref, oa_ref, ob_ref, m_ref, woa_ref, wob_ref, wout_ref, pmw_ref, pfw_ref,
                 wg_ref, wu_ref, wd_ref, pow_ref, out_ref, *, ff_chunk):
    d = x_ref.shape[1]
    ya = _dot(oa_ref[...], woa_ref[...])
    yb = _dot(ob_ref[...], wob_ref[...])
    merged = (_sigmoid(m_ref[:, :d].astype(F32)) * ya + _sigmoid(m_ref[:, d:].astype(F32)) * yb)
    mix = _dot(merged.astype(BF16), wout_ref[...])
    h = x_ref[...] + _rms(mix, pmw_ref[...])
    v = _rms(h, pfw_ref[...]).astype(BF16)
    d_ff = wg_ref.shape[1]
    f = None
    for c0 in range(0, d_ff, ff_chunk):
        gate = _dot(v, wg_ref[:, c0:c0 + ff_chunk])
        up = _dot(v, wu_ref[:, c0:c0 + ff_chunk])
        part = _dot((_silu(gate) * up).astype(BF16), wd_ref[c0:c0 + ff_chunk, :])
        f = part if f is None else f + part
    out_ref[...] = h + _rms(f, pow_ref[...])


def _post(x, oa, ob, m, woa, wob, wout, pmw, pfw, wg, wu, wd, pow_, tm, layer):
    n, d = x.shape
    d_ff = wg.shape[1]
    ff_chunk = d_ff // 2 if (d_ff // 2) % 128 == 0 else d_ff
    row = lambda i: (i, 0)
    const = lambda i: (0, 0)
    resident = dict(pipeline_mode=pl.Buffered(1))

    def wspec(w):
        return pl.BlockSpec(w.shape, const, **resident)

    return pl.pallas_call(
        functools.partial(_post_kernel, ff_chunk=ff_chunk),
        grid=(n // tm,),
        in_specs=[pl.BlockSpec((tm, d), row),
                  pl.BlockSpec((tm, oa.shape[1]), row),
                  pl.BlockSpec((tm, ob.shape[1]), row),
                  pl.BlockSpec((tm, m.shape[1]), row),
                  wspec(woa), wspec(wob), wspec(wout),
                  pl.BlockSpec((1, d), const), pl.BlockSpec((1, d), const),
                  wspec(wg), wspec(wu), wspec(wd),
                  pl.BlockSpec((1, d), const)],
        out_specs=pl.BlockSpec((tm, d), row),
        out_shape=jax.ShapeDtypeStruct((n, d), F32),
        compiler_params=pltpu.CompilerParams(dimension_semantics=("arbitrary",),
                                             vmem_limit_bytes=VMEM_LIMIT),
        name=f"post_l{layer}",
    )(x, oa, ob, m, woa, wob, wout, pmw, pfw, wg, wu, wd, pow_)


def _pick_tile(n, want):
    t = min(n, want)
    while n % t:
        t //= 2
    return t


def kernel(x, hg_lower_bounds, pre_mix_w, w_in, hg_norm_w, conv_w, dn_a_log, dn_dt_bias, dn_norm_w,
           w_o_hg, w_o_dn, w_out, post_mix_w, pre_ffn_w, w_gate_up, w_down, post_ffn_w):
    batch, seq, d = x.shape
    depth = w_in.shape[0]
    n = batch * seq
    W = HEADS_W
    d_ff = w_down.shape[1]
    assert seq % CHUNK == 0 and w_in.shape[2] == 8 * W + 2 * N_HEADS + 2 * d
    tm = _pick_tile(n, 512)

    h = x.reshape(n, d)
    lb_raw = hg_lower_bounds.astype(F32)
    for l in range(depth):
        wl = w_in[l]
        w_main = jnp.concatenate([wl[:, :8 * W], wl[:, 8 * W + 2 * N_HEADS:]], axis=1).astype(BF16)
        w_ab = jnp.pad(wl[:, 8 * W:8 * W + 2 * N_HEADS], ((0, 0), (0, 128 - 2 * N_HEADS))).astype(BF16)
        hq, hk, hb, hi, hg, dqkv, dg, ab, m = _inproj(
            h, pre_mix_w[l].reshape(1, d), w_main, w_ab, lb_raw, l, tm)

        oa = _hgrn(hq, hk, hb, hi, hg, hg_norm_w[l].reshape(1, HEAD_DIM), batch, seq, l)

        par = jnp.pad(jnp.stack([dn_a_log[l], dn_dt_bias[l]]).astype(F32),
                      ((0, 0), (0, 128 - N_HEADS)))
        ob = _gdn(dqkv, conv_w[l].astype(F32), ab, par, dg, dn_norm_w[l].reshape(1, HEAD_DIM),
                  batch, seq, l)

        wgu = w_gate_up[l]
        h = _post(h, oa, ob, m, w_o_hg[l].astype(BF16), w_o_dn[l].astype(BF16), w_out[l].astype(BF16),
                  post_mix_w[l].reshape(1, d), pre_ffn_w[l].reshape(1, d),
                  wgu[:, :d_ff].astype(BF16), wgu[:, d_ff:].astype(BF16), w_down[l].astype(BF16),
                  post_ffn_w[l].reshape(1, d), tm, l)
    return h.reshape(batch, seq, d)
```

```python
import functools

import jax
import jax.numpy as jnp
from jax import lax
from jax.experimental import pallas as pl
from jax.experimental.pallas import tpu as pltpu

F32 = jnp.float32
BF16 = jnp.bfloat16

EPS = 1e-6
CHUNK = 64
HEAD_DIM = 128
LANES = 128
N_HEADS = 4
HEADS_W = N_HEADS * HEAD_DIM
CONV_W = 4
SUBLANES = 8
HGRN_GROUP = 4
GDN_GROUP = 32
POST_ROW_PARTS = 2
VMEM_LIMIT = 56 * 1024 * 1024


def _dot(a, b):
    return jnp.dot(a, b, preferred_element_type=F32)


def _dot_nt(a, b):
    return lax.dot_general(a, b, (((1,), (1,)), ((), ())), preferred_element_type=F32)


def _dot_tn(a, b):
    return lax.dot_general(a, b, (((0,), (0,)), ((), ())), preferred_element_type=F32)


def _sigmoid(x):
    return 1.0 / (1.0 + jnp.exp(-x))


def _silu(x):
    return x * _sigmoid(x)


def _rms(x, w):
    return x * lax.rsqrt(jnp.mean(x * x, axis=-1, keepdims=True) + EPS) * w


def _tril_ones(n, dtype):
    r = lax.broadcasted_iota(jnp.int32, (n, n), 0)
    c = lax.broadcasted_iota(jnp.int32, (n, n), 1)
    return jnp.where(c <= r, 1.0, 0.0).astype(dtype)


def _chunk_cumsum(x, tril):
    hi = x.astype(BF16)
    lo = (x - hi.astype(F32)).astype(BF16)
    return _dot(tril, hi) + _dot(tril, lo)


def _group_row_bcast(x, group, row):
    n, w = x.shape
    x3 = x.reshape(n // group, group, w)
    return jnp.broadcast_to(x3[:, row:row + 1, :], x3.shape).reshape(n, w)


def _head_cols(h):
    return slice(h * HEAD_DIM, (h + 1) * HEAD_DIM)


def _inproj_kernel(x_ref, pw_ref, w_ref, wab_ref, lbraw_ref, cw_ref, par_ref,
                   hq_ref, hk_ref, hb_ref, hi_ref, hg_ref, dq_ref, dk_ref, dv_ref, dg_ref, gb_ref, m_ref,
                   halo_ref, *, layer, tm, tiles_per_seq):
    W = HEADS_W

    @pl.when(pl.program_id(0) % tiles_per_seq == 0)
    def _():
        halo_ref[...] = jnp.zeros_like(halo_ref)

    x = x_ref[...]
    u = _rms(x, pw_ref[...]).astype(BF16)
    tril = _tril_ones(CHUNK, BF16)
    row8 = lax.broadcasted_iota(jnp.int32, (SUBLANES, W), 0)

    def conv_silu(p, g):
        cols = slice(g * W, (g + 1) * W)
        halo = halo_ref[:, cols]
        halo_ref[:, cols] = p[tm - SUBLANES:tm, :]
        acc = p * cw_ref[CONV_W - 1:CONV_W, cols]
        for s in range(1, CONV_W):
            r = pltpu.roll(p, s, axis=0)
            head = jnp.where(row8 < s, pltpu.roll(halo, s, axis=0), r[0:SUBLANES, :])
            r = jnp.concatenate([head, r[SUBLANES:, :]], axis=0)
            acc = acc + r * cw_ref[CONV_W - 1 - s:CONV_W - s, cols]
        return _silu(acc)

    def l2norm_store(y, ref, scale):
        for h in range(N_HEADS):
            cs = _head_cols(h)
            yh = y[:, cs]
            ref[:, cs] = (yh * (lax.rsqrt(jnp.sum(yh * yh, axis=-1, keepdims=True) + EPS) * scale)).astype(BF16)

    def epi_dq(p):
        l2norm_store(conv_silu(p, 0), dq_ref, HEAD_DIM ** -0.5)

    def epi_dk(p):
        l2norm_store(conv_silu(p, 1), dk_ref, 1.0)

    def epi_dv(p):
        dv_ref[...] = conv_silu(p, 2).astype(BF16)

    def epi_cast(ref, c0=0):
        def epi(p):
            ref[:, c0:c0 + p.shape[1]] = p.astype(BF16)
        return epi

    def epi_f(f):
        depth = lbraw_ref.shape[0]
        rows = [lbraw_ref[r:r + 1, :] for r in range(depth)]
        mx = functools.reduce(jnp.maximum, rows)
        es = [jnp.exp(r - mx) for r in rows]
        tot = functools.reduce(lambda a, b: a + b, es)
        cum = functools.reduce(lambda a, b: a + b, es[:layer + 1])
        lb = (cum - es[0]) / tot
        e = jnp.exp(-jnp.abs(f))
        log_sig = jnp.minimum(f, 0.0) - jnp.log1p(e)
        a = jnp.log(lb)
        c = jnp.log1p(-lb) + log_sig
        log_f = jnp.maximum(a, c) + jnp.log1p(jnp.exp(-jnp.abs(a - c)))
        inv = 1.0 / (1.0 + e)
        hk_ref[...] = ((1.0 - lb) * jnp.where(f >= 0.0, e * inv, inv)).astype(BF16)
        for c0 in range(0, tm, CHUNK):
            hb_ref[c0:c0 + CHUNK, :] = _chunk_cumsum(log_f[c0:c0 + CHUNK, :], tril)

    def epi_ab(ab):
        z = ab + par_ref[1:2, :]
        softplus = jnp.maximum(z, 0.0) + jnp.log1p(jnp.exp(-jnp.abs(z)))
        g_all = -jnp.exp(par_ref[0:1, :]) * softplus
        beta = _sigmoid(ab)
        lane = lax.broadcasted_iota(jnp.int32, (CHUNK, LANES), 1)
        for c0 in range(0, tm, CHUNK):
            gc = _chunk_cumsum(g_all[c0:c0 + CHUNK, :], tril)
            gb_ref[c0:c0 + CHUNK, :] = jnp.where(lane < N_HEADS, gc, beta[c0:c0 + CHUNK, :])

    groups = [(w_ref, 4 * W, epi_dq), (w_ref, 0, epi_cast(hq_ref)),
              (w_ref, 5 * W, epi_dk), (w_ref, 2 * W, epi_cast(hi_ref)),
              (w_ref, 6 * W, epi_dv), (w_ref, 3 * W, epi_cast(hg_ref)),
              (w_ref, W, epi_f), (w_ref, 7 * W, epi_cast(dg_ref))]
    groups += [(w_ref, 8 * W + c0, epi_cast(m_ref, c0)) for c0 in range(0, m_ref.shape[1], W)]
    groups.insert(9, (wab_ref, 0, epi_ab))
    pending = None
    for wref, c0, epi in groups:
        width = min(W, wref.shape[1] - c0)
        r = _dot(u, wref[:, c0:c0 + width])
        if pending is not None:
            pending[0](pending[1])
        pending = (epi, r)
    pending[0](pending[1])


def _inproj(x, pre_w, w_main, w_ab, lb_raw, conv_w, par, layer, tm, seq):
    n, d = x.shape
    wm = w_main.shape[1]
    m_w = wm - 8 * HEADS_W
    row = lambda i: (i, 0)
    const = lambda i: (0, 0)
    resident = dict(pipeline_mode=pl.Buffered(1))
    out_shapes = [
        jax.ShapeDtypeStruct((n, HEADS_W), BF16),
        jax.ShapeDtypeStruct((n, HEADS_W), BF16),
        jax.ShapeDtypeStruct((n, HEADS_W), F32),
        jax.ShapeDtypeStruct((n, HEADS_W), BF16),
        jax.ShapeDtypeStruct((n, HEADS_W), BF16),
        jax.ShapeDtypeStruct((n, HEADS_W), BF16),
        jax.ShapeDtypeStruct((n, HEADS_W), BF16),
        jax.ShapeDtypeStruct((n, HEADS_W), BF16),
        jax.ShapeDtypeStruct((n, HEADS_W), BF16),
        jax.ShapeDtypeStruct((n, LANES), F32),
        jax.ShapeDtypeStruct((n, m_w), BF16),
    ]
    out_specs = [pl.BlockSpec((tm, s.shape[1]), row) for s in out_shapes]
    return pl.pallas_call(
        functools.partial(_inproj_kernel, layer=layer, tm=tm, tiles_per_seq=seq // tm),
        grid=(n // tm,),
        in_specs=[
            pl.BlockSpec((tm, d), row),
            pl.BlockSpec((1, d), const),
            pl.BlockSpec((d, wm), const, **resident),
            pl.BlockSpec((d, LANES), const, **resident),
            pl.BlockSpec(lb_raw.shape, const),
            pl.BlockSpec(conv_w.shape, const),
            pl.BlockSpec(par.shape, const),
        ],
        out_specs=out_specs,
        out_shape=out_shapes,
        scratch_shapes=[pltpu.VMEM((SUBLANES, 3 * HEADS_W), F32)],
        compiler_params=pltpu.CompilerParams(dimension_semantics=("arbitrary",),
                                             vmem_limit_bytes=VMEM_LIMIT),
        name=f"inproj_l{layer}",
    )(x, pre_w, w_main, w_ab, lb_raw, conv_w, par)


def _hgrn_kernel(q_ref, k_ref, b_ref, v_ref, o_ref, st_ref, *, nb, group):
    C = CHUNK

    @pl.when(pl.program_id(0) == 0)
    def _():
        st_ref[...] = jnp.zeros_like(st_ref)

    row = lax.broadcasted_iota(jnp.int32, (C, C), 0)
    col = lax.broadcasted_iota(jnp.int32, (C, C), 1)
    levels = (32, 16, 8)
    level_masks = [((row // m) % 2 == 1) & ((col // m) == (row // m) - 1) for m in levels]
    diag_masks = [(col == (row // SUBLANES) * SUBLANES + jj) & (jj <= row % SUBLANES)
                  for jj in range(SUBLANES)]
    ones = jnp.ones((HEAD_DIM, C), BF16)

    streams = [(bi, h) for bi in range(nb) for h in range(N_HEADS)]
    for g0 in range(0, len(streams), group):
        grp = streams[g0:g0 + group]
        n = len(grp)
        q = [q_ref[bi, :, _head_cols(h)].astype(F32) for bi, h in grp]
        k = [k_ref[bi, :, _head_cols(h)].astype(F32) for bi, h in grp]
        v = [v_ref[bi, :, _head_cols(h)] for bi, h in grp]
        b = [b_ref[bi, :, _head_cols(h)] for bi, h in grp]
        st = [st_ref[bi * N_HEADS + h] for bi, h in grp]

        attn = [jnp.zeros((C, C), F32) for _ in range(n)]
        for m, mask in zip(levels, level_masks):
            for i in range(n):
                be = _group_row_bcast(b[i], 2 * m, m - 1)
                x = jnp.exp(-jnp.abs(b[i] - be))
                s = _dot_nt((q[i] * x).astype(BF16), (k[i] * x).astype(BF16))
                attn[i] = jnp.where(mask, s, attn[i])
        for i in range(n):
            ps = []
            for jj in range(SUBLANES):
                bj = _group_row_bcast(b[i], SUBLANES, jj)
                kj = _group_row_bcast(k[i], SUBLANES, jj)
                ps.append((q[i] * kj * jnp.exp(jnp.minimum(b[i] - bj, 0.0))).astype(BF16))
            rs = _dot(jnp.concatenate(ps, axis=0), ones)
            for jj in range(SUBLANES):
                attn[i] = jnp.where(diag_masks[jj], rs[jj * C:(jj + 1) * C, :], attn[i])

        o = [_dot_nt((q[i] * jnp.exp(b[i])).astype(BF16), st[i].astype(BF16)) for i in range(n)]
        o = [o[i] + _dot(attn[i].astype(BF16), v[i]) for i in range(n)]
        for i, (bi, h) in enumerate(grp):
            b_last = b[i][C - 1:C, :]
            kd = (k[i] * jnp.exp(b_last - b[i])).astype(BF16)
            st_ref[bi * N_HEADS + h] = st[i] * jnp.exp(b_last) + _dot_tn(v[i], kd)
            o_ref[bi, :, _head_cols(h)] = o[i].astype(BF16)


def _hgrn(q, k, b, v, batch, seq, layer):
    blk = lambda t: (0, t, 0)
    spec = pl.BlockSpec((batch, CHUNK, HEADS_W), blk)
    r3 = lambda a: a.reshape(batch, seq, a.shape[-1])
    out = pl.pallas_call(
        functools.partial(_hgrn_kernel, nb=batch, group=HGRN_GROUP),
        grid=(seq // CHUNK,),
        in_specs=[spec, spec, spec, spec],
        out_specs=spec,
        out_shape=jax.ShapeDtypeStruct((batch, seq, HEADS_W), BF16),
        scratch_shapes=[pltpu.VMEM((batch * N_HEADS, HEAD_DIM, HEAD_DIM), F32)],
        compiler_params=pltpu.CompilerParams(dimension_semantics=("arbitrary",),
                                             vmem_limit_bytes=VMEM_LIMIT),
        name=f"hgrn_l{layer}",
    )(r3(q), r3(k), r3(b), r3(v))
    return out.reshape(batch * seq, HEADS_W)


def _gdn_kernel(q_ref, k_ref, v_ref, gb_ref, o_ref, s_ref, *, nb, group):
    C = CHUNK

    @pl.when(pl.program_id(0) == 0)
    def _():
        s_ref[...] = jnp.zeros_like(s_ref)

    row = lax.broadcasted_iota(jnp.int32, (C, C), 0)
    col = lax.broadcasted_iota(jnp.int32, (C, C), 1)
    causal = col <= row
    strict = col < row
    eye = jnp.where(col == row, 1.0, 0.0).astype(F32)

    streams = [(bi, h) for bi in range(nb) for h in range(N_HEADS)]
    for g0 in range(0, len(streams), group):
        grp = streams[g0:g0 + group]
        n = len(grp)
        q = [q_ref[bi, :, _head_cols(h)].astype(F32) for bi, h in grp]
        kbf = [k_ref[bi, :, _head_cols(h)] for bi, h in grp]
        k = [kbf[i].astype(F32) for i in range(n)]
        v = [v_ref[bi, :, _head_cols(h)].astype(F32) for bi, h in grp]
        s = [s_ref[bi * N_HEADS + h] for bi, h in grp]
        beta, gcb = [], []
        for bi, h in grp:
            gb = gb_ref[bi]
            gcb.append(jnp.broadcast_to(gb[:, h:h + 1], (C, HEAD_DIM)))
            beta.append(jnp.broadcast_to(gb[:, N_HEADS + h:N_HEADS + h + 1], (C, HEAD_DIM)))

        eg = [jnp.exp(gcb[i]) for i in range(n)]
        kb = [k[i] * beta[i] for i in range(n)]
        decay = []
        for i in range(n):
            g_row = jnp.transpose(gcb[i])[:C, :]
            decay.append(jnp.exp(jnp.minimum(gcb[i][:, :C] - g_row, 0.0)))
        qk = [_dot_nt(jnp.concatenate([q[i], kb[i]], axis=0).astype(BF16), kbf[i])
              for i in range(n)]
        attn = [jnp.where(causal, qk[i][:C] * decay[i], 0.0) for i in range(n)]
        a = [jnp.where(strict, qk[i][C:] * decay[i], 0.0) for i in range(n)]

        t = [eye - a[i] for i in range(n)]
        p = a
        for _ in range(5):
            pb = [p[i].astype(BF16) for i in range(n)]
            p = [_dot(pb[i], pb[i]) for i in range(n)]
            t = [t[i] + _dot(t[i].astype(BF16), p[i].astype(BF16)) for i in range(n)]
        uw = [_dot(t[i].astype(BF16),
                   jnp.concatenate([v[i] * beta[i], kb[i] * eg[i]], axis=1).astype(BF16)) for i in range(n)]
        ws = [_dot(jnp.concatenate([uw[i][:, HEAD_DIM:], q[i] * eg[i]], axis=0).astype(BF16),
                   s[i].astype(BF16)) for i in range(n)]
        vnb = [(uw[i][:, :HEAD_DIM] - ws[i][:C]).astype(BF16) for i in range(n)]
        o = [ws[i][C:] + _dot(attn[i].astype(BF16), vnb[i]) for i in range(n)]
        for i, (bi, h) in enumerate(grp):
            g_last = gcb[i][C - 1:C, :]
            kd = (k[i] * jnp.exp(g_last - gcb[i])).astype(BF16)
            s_ref[bi * N_HEADS + h] = s[i] * jnp.exp(g_last) + _dot_tn(kd, vnb[i])
            o_ref[bi, :, _head_cols(h)] = o[i].astype(BF16)


def _gdn(q, k, v, gb, batch, seq, layer):
    blk = lambda t: (0, t, 0)
    spec = pl.BlockSpec((batch, CHUNK, HEADS_W), blk)
    r3 = lambda a: a.reshape(batch, seq, a.shape[-1])
    out = pl.pallas_call(
        functools.partial(_gdn_kernel, nb=batch, group=GDN_GROUP),
        grid=(seq // CHUNK,),
        in_specs=[spec, spec, spec, pl.BlockSpec((batch, CHUNK, LANES), blk)],
        out_specs=spec,
        out_shape=jax.ShapeDtypeStruct((batch, seq, HEADS_W), BF16),
        scratch_shapes=[pltpu.VMEM((batch * N_HEADS, HEAD_DIM, HEAD_DIM), F32)],
        compiler_params=pltpu.CompilerParams(dimension_semantics=("arbitrary",),
                                             vmem_limit_bytes=VMEM_LIMIT),
        name=f"gdn_l{layer}",
    )(r3(q), r3(k), r3(v), r3(gb))
    return out.reshape(batch * seq, HEADS_W)


def _head_norm_gate(o_ref, g_ref, w_ref, rows):
    outs = []
    for h in range(N_HEADS):
        cs = _head_cols(h)
        o = _rms(o_ref[rows, cs].astype(F32), w_ref[:, cs]) * _silu(g_ref[rows, cs].astype(F32))
        outs.append(o.astype(BF16))
    return jnp.concatenate(outs, axis=1)


def _post_kernel(x_ref, oa_ref, ga_ref, nwa_ref, ob_ref, gb_ref, nwb_ref, m_ref, woa_ref, wob_ref, wout_ref,
                 pmw_ref, pfw_ref, wg_ref, wu_ref, wd_ref, pow_ref, out_ref, *, ff_chunk, row_parts):
    d = x_ref.shape[1]
    d_ff = wg_ref.shape[1]
    part_rows = x_ref.shape[0] // row_parts

    def row_part(r0):
        rows = slice(r0, r0 + part_rows)
        ya = _dot(_head_norm_gate(oa_ref, ga_ref, nwa_ref, rows), woa_ref[...])
        yield
        yb = _dot(_head_norm_gate(ob_ref, gb_ref, nwb_ref, rows), wob_ref[...])
        yield
        merged = (_sigmoid(m_ref[rows, :d].astype(F32)) * ya + _sigmoid(m_ref[rows, d:].astype(F32)) * yb)
        mix = _dot(merged.astype(BF16), wout_ref[...])
        yield
        h = x_ref[rows, :] + _rms(mix, pmw_ref[...])
        v = _rms(h, pfw_ref[...]).astype(BF16)
        f = None
        for c0 in range(0, d_ff, ff_chunk):
            gate = _dot(v, wg_ref[:, c0:c0 + ff_chunk])
            up = _dot(v, wu_ref[:, c0:c0 + ff_chunk])
            yield
            part = _dot((_silu(gate) * up).astype(BF16), wd_ref[c0:c0 + ff_chunk, :])
            f = part if f is None else f + part
            yield
        out_ref[rows, :] = h + _rms(f, pow_ref[...])

    parts = [row_part(i * part_rows) for i in range(row_parts)]
    while parts:
        parts = [p for p in parts if next(p, StopIteration) is not StopIteration]


def _post(x, oa, ga, nwa, ob, gb, nwb, m, woa, wob, wout, pmw, pfw, wg, wu, wd, pow_, tm, layer):
    n, d = x.shape
    d_ff = wg.shape[1]
    ff_chunk = d_ff // 2 if (d_ff // 2) % LANES == 0 else d_ff
    row = lambda i: (i, 0)
    const = lambda i: (0, 0)
    resident = dict(pipeline_mode=pl.Buffered(1))

    def wspec(w):
        return pl.BlockSpec(w.shape, const, **resident)

    def tspec(a):
        return pl.BlockSpec((tm, a.shape[1]), row)

    def vspec(a):
        return pl.BlockSpec(a.shape, const)

    return pl.pallas_call(
        functools.partial(_post_kernel, ff_chunk=ff_chunk, row_parts=POST_ROW_PARTS),
        grid=(n // tm,),
        in_specs=[tspec(x), tspec(oa), tspec(ga), vspec(nwa), tspec(ob), tspec(gb), vspec(nwb), tspec(m),
                  wspec(woa), wspec(wob), wspec(wout), vspec(pmw), vspec(pfw),
                  wspec(wg), wspec(wu), wspec(wd), vspec(pow_)],
        out_specs=pl.BlockSpec((tm, d), row),
        out_shape=jax.ShapeDtypeStruct((n, d), F32),
        compiler_params=pltpu.CompilerParams(dimension_semantics=("arbitrary",),
                                             vmem_limit_bytes=VMEM_LIMIT),
        name=f"post_l{layer}",
    )(x, oa, ga, nwa, ob, gb, nwb, m, woa, wob, wout, pmw, pfw, wg, wu, wd, pow_)


def _pick_tile(n, want):
    t = min(n, want)
    while n % t:
        t //= 2
    return t


def kernel(x, hg_lower_bounds, pre_mix_w, w_in, hg_norm_w, conv_w, dn_a_log, dn_dt_bias, dn_norm_w,
           w_o_hg, w_o_dn, w_out, post_mix_w, pre_ffn_w, w_gate_up, w_down, post_ffn_w):
    batch, seq, d = x.shape
    depth = w_in.shape[0]
    n = batch * seq
    W = HEADS_W
    d_ff = w_down.shape[1]
    assert seq % CHUNK == 0 and w_in.shape[2] == 8 * W + 2 * N_HEADS + 2 * d
    tm = _pick_tile(seq, 512)
    assert tm % CHUNK == 0

    h = x.reshape(n, d)
    lb_raw = hg_lower_bounds.astype(F32)
    for l in range(depth):
        wl = w_in[l]
        w_main = jnp.concatenate([wl[:, :8 * W], wl[:, 8 * W + 2 * N_HEADS:]], axis=1).astype(BF16)
        w_ab = jnp.pad(wl[:, 8 * W:8 * W + 2 * N_HEADS], ((0, 0), (0, LANES - 2 * N_HEADS))).astype(BF16)
        par = jnp.pad(jnp.stack([dn_a_log[l], dn_dt_bias[l]]).astype(F32), ((0, 0), (0, LANES - N_HEADS)))
        hq, hk, hb, hi, hg, dq, dk, dv, dg, gb, m = _inproj(
            h, pre_mix_w[l].reshape(1, d), w_main, w_ab, lb_raw, conv_w[l].astype(F32), par, l, tm, seq)

        oa = _hgrn(hq, hk, hb, hi, batch, seq, l)
        ob = _gdn(dq, dk, dv, gb, batch, seq, l)

        wgu = w_gate_up[l]
        h = _post(h, oa, hg, jnp.tile(hg_norm_w[l], N_HEADS).reshape(1, W),
                  ob, dg, jnp.tile(dn_norm_w[l], N_HEADS).reshape(1, W), m,
                  w_o_hg[l].astype(BF16), w_o_dn[l].astype(BF16), w_out[l].astype(BF16),
                  post_mix_w[l].reshape(1, d), pre_ffn_w[l].reshape(1, d),
                  wgu[:, :d_ff].astype(BF16), wgu[:, d_ff:].astype(BF16), w_down[l].astype(BF16),
                  post_ffn_w[l].reshape(1, d), tm, l)
    return h.reshape(batch, seq, d)
```

```python
import functools

import jax
import jax.numpy as jnp
from jax import lax
from jax.experimental import pallas as pl
from jax.experimental.pallas import tpu as pltpu

F32 = jnp.float32
BF16 = jnp.bfloat16

EPS = 1e-6
CHUNK = 64
HEAD_DIM = 128
LANES = 128
N_HEADS = 4
HEADS_W = N_HEADS * HEAD_DIM
CONV_W = 4
SUBLANES = 8
HGRN_GROUP = 4
GDN_GROUP = 32
POST_ROW_PARTS = 2
VMEM_LIMIT = 56 * 1024 * 1024


def _dot(a, b):
    return jnp.dot(a, b, preferred_element_type=F32)


def _dot_nt(a, b):
    return lax.dot_general(a, b, (((1,), (1,)), ((), ())), preferred_element_type=F32)


def _dot_tn(a, b):
    return lax.dot_general(a, b, (((0,), (0,)), ((), ())), preferred_element_type=F32)


def _sigmoid(x):
    return 1.0 / (1.0 + jnp.exp(-x))


def _silu(x):
    return x * _sigmoid(x)


def _rms(x, w):
    return x * lax.rsqrt(jnp.mean(x * x, axis=-1, keepdims=True) + EPS) * w


def _tril_ones(n, dtype):
    r = lax.broadcasted_iota(jnp.int32, (n, n), 0)
    c = lax.broadcasted_iota(jnp.int32, (n, n), 1)
    return jnp.where(c <= r, 1.0, 0.0).astype(dtype)


def _chunk_cumsum(x, tril):
    hi = x.astype(BF16)
    lo = (x - hi.astype(F32)).astype(BF16)
    return _dot(tril, hi) + _dot(tril, lo)


def _group_row_bcast(x, group, row):
    n, w = x.shape
    x3 = x.reshape(n // group, group, w)
    return jnp.broadcast_to(x3[:, row:row + 1, :], x3.shape).reshape(n, w)


def _head_cols(h):
    return slice(h * HEAD_DIM, (h + 1) * HEAD_DIM)


def _inproj_kernel(x_ref, pw_ref, w_ref, wab_ref, lbraw_ref, cw_ref, par_ref,
                   hq_ref, hk_ref, hb_ref, hi_ref, hg_ref, dq_ref, dk_ref, dv_ref, dg_ref, gb_ref, m_ref,
                   halo_ref, *, layer, tm, tiles_per_seq):
    W = HEADS_W

    @pl.when(pl.program_id(0) % tiles_per_seq == 0)
    def _():
        halo_ref[...] = jnp.zeros_like(halo_ref)

    x = x_ref[...]
    u = _rms(x, pw_ref[...]).astype(BF16)
    tril = _tril_ones(CHUNK, BF16)
    row8 = lax.broadcasted_iota(jnp.int32, (SUBLANES, W), 0)

    def conv_silu(p, g):
        cols = slice(g * W, (g + 1) * W)
        halo = halo_ref[:, cols]
        halo_ref[:, cols] = p[tm - SUBLANES:tm, :]
        acc = p * cw_ref[CONV_W - 1:CONV_W, cols]
        for s in range(1, CONV_W):
            r = pltpu.roll(p, s, axis=0)
            head = jnp.where(row8 < s, pltpu.roll(halo, s, axis=0), r[0:SUBLANES, :])
            r = jnp.concatenate([head, r[SUBLANES:, :]], axis=0)
            acc = acc + r * cw_ref[CONV_W - 1 - s:CONV_W - s, cols]
        return _silu(acc)

    def l2norm_store(y, ref, scale):
        for h in range(N_HEADS):
            cs = _head_cols(h)
            yh = y[:, cs]
            ref[:, cs] = (yh * (lax.rsqrt(jnp.sum(yh * yh, axis=-1, keepdims=True) + EPS) * scale)).astype(BF16)

    def epi_dq(p):
        l2norm_store(conv_silu(p, 0), dq_ref, HEAD_DIM ** -0.5)

    def epi_dk(p):
        l2norm_store(conv_silu(p, 1), dk_ref, 1.0)

    def epi_dv(p):
        dv_ref[...] = conv_silu(p, 2).astype(BF16)

    def epi_cast(ref, c0=0):
        def epi(p):
            ref[:, c0:c0 + p.shape[1]] = p.astype(BF16)
        return epi

    def epi_f(f):
        depth = lbraw_ref.shape[0]
        rows = [lbraw_ref[r:r + 1, :] for r in range(depth)]
        mx = functools.reduce(jnp.maximum, rows)
        es = [jnp.exp(r - mx) for r in rows]
        tot = functools.reduce(lambda a, b: a + b, es)
        cum = functools.reduce(lambda a, b: a + b, es[:layer + 1])
        lb = (cum - es[0]) / tot
        e = jnp.exp(-jnp.abs(f))
        log_sig = jnp.minimum(f, 0.0) - jnp.log(1.0 + e)
        a = jnp.log(lb)
        c = jnp.log1p(-lb) + log_sig
        log_f = jnp.maximum(a, c) + jnp.log(1.0 + jnp.exp(-jnp.abs(a - c)))
        inv = 1.0 / (1.0 + e)
        hk_ref[...] = ((1.0 - lb) * jnp.where(f >= 0.0, e * inv, inv)).astype(BF16)
        for c0 in range(0, tm, CHUNK):
            hb_ref[c0:c0 + CHUNK, :] = _chunk_cumsum(log_f[c0:c0 + CHUNK, :], tril)

    def epi_ab(ab):
        z = ab + par_ref[1:2, :]
        softplus = jnp.maximum(z, 0.0) + jnp.log(1.0 + jnp.exp(-jnp.abs(z)))
        g_all = -jnp.exp(par_ref[0:1, :]) * softplus
        beta = _sigmoid(ab)
        lane = lax.broadcasted_iota(jnp.int32, (CHUNK, LANES), 1)
        for c0 in range(0, tm, CHUNK):
            gc = _chunk_cumsum(g_all[c0:c0 + CHUNK, :], tril)
            gb_ref[c0:c0 + CHUNK, :] = jnp.where(lane < N_HEADS, gc, beta[c0:c0 + CHUNK, :])

    groups = [(w_ref, 4 * W, epi_dq), (w_ref, 0, epi_cast(hq_ref)),
              (w_ref, 5 * W, epi_dk), (w_ref, 2 * W, epi_cast(hi_ref)),
              (w_ref, 6 * W, epi_dv), (w_ref, 3 * W, epi_cast(hg_ref)),
              (w_ref, W, epi_f), (w_ref, 7 * W, epi_cast(dg_ref))]
    groups += [(w_ref, 8 * W + c0, epi_cast(m_ref, c0)) for c0 in range(0, m_ref.shape[1], W)]
    groups.insert(9, (wab_ref, 0, epi_ab))
    pending = None
    for wref, c0, epi in groups:
        width = min(W, wref.shape[1] - c0)
        r = _dot(u, wref[:, c0:c0 + width])
        if pending is not None:
            pending[0](pending[1])
        pending = (epi, r)
    pending[0](pending[1])


def _inproj(x, pre_w, w_main, w_ab, lb_raw, conv_w, par, layer, tm, seq):
    n, d = x.shape
    wm = w_main.shape[1]
    m_w = wm - 8 * HEADS_W
    row = lambda i: (i, 0)
    const = lambda i: (0, 0)
    resident = dict(pipeline_mode=pl.Buffered(1))
    out_shapes = [
        jax.ShapeDtypeStruct((n, HEADS_W), BF16),
        jax.ShapeDtypeStruct((n, HEADS_W), BF16),
        jax.ShapeDtypeStruct((n, HEADS_W), F32),
        jax.ShapeDtypeStruct((n, HEADS_W), BF16),
        jax.ShapeDtypeStruct((n, HEADS_W), BF16),
        jax.ShapeDtypeStruct((n, HEADS_W), BF16),
        jax.ShapeDtypeStruct((n, HEADS_W), BF16),
        jax.ShapeDtypeStruct((n, HEADS_W), BF16),
        jax.ShapeDtypeStruct((n, HEADS_W), BF16),
        jax.ShapeDtypeStruct((n, LANES), F32),
        jax.ShapeDtypeStruct((n, m_w), BF16),
    ]
    out_specs = [pl.BlockSpec((tm, s.shape[1]), row) for s in out_shapes]
    return pl.pallas_call(
        functools.partial(_inproj_kernel, layer=layer, tm=tm, tiles_per_seq=seq // tm),
        grid=(n // tm,),
        in_specs=[
            pl.BlockSpec((tm, d), row),
            pl.BlockSpec((1, d), const),
            pl.BlockSpec((d, wm), const, **resident),
            pl.BlockSpec((d, LANES), const, **resident),
            pl.BlockSpec(lb_raw.shape, const),
            pl.BlockSpec(conv_w.shape, const),
            pl.BlockSpec(par.shape, const),
        ],
        out_specs=out_specs,
        out_shape=out_shapes,
        scratch_shapes=[pltpu.VMEM((SUBLANES, 3 * HEADS_W), F32)],
        compiler_params=pltpu.CompilerParams(dimension_semantics=("arbitrary",),
                                             vmem_limit_bytes=VMEM_LIMIT),
        name=f"inproj_l{layer}",
    )(x, pre_w, w_main, w_ab, lb_raw, conv_w, par)


def _row_bcast_load(ref, stream, group, row):
    bi, h = stream
    return jnp.concatenate(
        [jnp.broadcast_to(ref[bi, g0 + row:g0 + row + 1, _head_cols(h)], (group, HEAD_DIM))
         for g0 in range(0, CHUNK, group)], axis=0)


def _hgrn_steps(q_ref, k_ref, b_ref, v_ref, o_ref, st_ref, *, nb, group):
    C = CHUNK
    row = lax.broadcasted_iota(jnp.int32, (C, C), 0)
    col = lax.broadcasted_iota(jnp.int32, (C, C), 1)
    levels = (32, 16, 8)
    level_masks = [((row // m) % 2 == 1) & ((col // m) == (row // m) - 1) for m in levels]
    diag_masks = [(col == (row // SUBLANES) * SUBLANES + jj) & (jj <= row % SUBLANES)
                  for jj in range(SUBLANES)]
    ones = jnp.ones((HEAD_DIM, C), BF16)

    streams = [(bi, h) for bi in range(nb) for h in range(N_HEADS)]
    for g0 in range(0, len(streams), group):
        grp = streams[g0:g0 + group]
        n = len(grp)
        q = [q_ref[bi, :, _head_cols(h)].astype(F32) for bi, h in grp]
        k = [k_ref[bi, :, _head_cols(h)].astype(F32) for bi, h in grp]
        v = [v_ref[bi, :, _head_cols(h)] for bi, h in grp]
        b = [b_ref[bi, :, _head_cols(h)] for bi, h in grp]
        st = [st_ref[bi * N_HEADS + h] for bi, h in grp]

        attn = [jnp.zeros((C, C), F32) for _ in range(n)]
        for m, mask in zip(levels, level_masks):
            for i in range(n):
                be = _row_bcast_load(b_ref, grp[i], 2 * m, m - 1)
                x = jnp.exp(-jnp.abs(b[i] - be))
                s = _dot_nt((q[i] * x).astype(BF16), (k[i] * x).astype(BF16))
                attn[i] = jnp.where(mask, s, attn[i])
        for i in range(n):
            ps = []
            for jj in range(SUBLANES):
                bj = _row_bcast_load(b_ref, grp[i], SUBLANES, jj)
                kj = _group_row_bcast(k[i], SUBLANES, jj)
                ps.append((q[i] * kj * jnp.exp(b[i] - bj)).astype(BF16))
            rs = _dot(jnp.concatenate(ps, axis=0), ones)
            for jj in range(SUBLANES):
                attn[i] = jnp.where(diag_masks[jj], rs[jj * C:(jj + 1) * C, :], attn[i])

        o = [_dot_nt((q[i] * jnp.exp(b[i])).astype(BF16), st[i].astype(BF16)) for i in range(n)]
        o = [o[i] + _dot(attn[i].astype(BF16), v[i]) for i in range(n)]
        for i, (bi, h) in enumerate(grp):
            b_last = b_ref[bi, C - 1:C, _head_cols(h)]
            kd = (k[i] * jnp.exp(b_last - b[i])).astype(BF16)
            st_ref[bi * N_HEADS + h] = st[i] * jnp.exp(b_last) + _dot_tn(v[i], kd)
            o_ref[bi, :, _head_cols(h)] = o[i].astype(BF16)
        yield


def _gdn_steps(q_ref, k_ref, v_ref, gb_ref, o_ref, s_ref, *, nb, group):
    C = CHUNK
    row = lax.broadcasted_iota(jnp.int32, (C, C), 0)
    col = lax.broadcasted_iota(jnp.int32, (C, C), 1)
    causal = col <= row
    strict = col < row
    eye = jnp.where(col == row, 1.0, 0.0).astype(F32)

    streams = [(bi, h) for bi in range(nb) for h in range(N_HEADS)]
    for g0 in range(0, len(streams), group):
        grp = streams[g0:g0 + group]
        n = len(grp)
        q = [q_ref[bi, :, _head_cols(h)].astype(F32) for bi, h in grp]
        kbf = [k_ref[bi, :, _head_cols(h)] for bi, h in grp]
        k = [kbf[i].astype(F32) for i in range(n)]
        v = [v_ref[bi, :, _head_cols(h)].astype(F32) for bi, h in grp]
        s = [s_ref[bi * N_HEADS + h] for bi, h in grp]
        beta, gcb = [], []
        for bi, h in grp:
            gb = gb_ref[bi]
            gcb.append(jnp.broadcast_to(gb[:, h:h + 1], (C, HEAD_DIM)))
            beta.append(jnp.broadcast_to(gb[:, N_HEADS + h:N_HEADS + h + 1], (C, HEAD_DIM)))

        eg = [jnp.exp(gcb[i]) for i in range(n)]
        kb = [k[i] * beta[i] for i in range(n)]
        decay = []
        for i in range(n):
            g_row = jnp.transpose(gcb[i])[:C, :]
            decay.append(jnp.exp(jnp.minimum(gcb[i][:, :C] - g_row, 0.0)))
        qk = [_dot_nt(jnp.concatenate([q[i], kb[i]], axis=0).astype(BF16), kbf[i])
              for i in range(n)]
        yield
        attn = [jnp.where(causal, qk[i][:C] * decay[i], 0.0) for i in range(n)]
        a = [jnp.where(strict, qk[i][C:] * decay[i], 0.0) for i in range(n)]

        t = [eye - a[i] for i in range(n)]
        ab16 = [a[i].astype(BF16) for i in range(n)]
        p = [_dot(ab16[i], ab16[i]) for i in range(n)]
        yield
        for step in range(5):
            last = step == 4
            pb = [p[i].astype(BF16) for i in range(n)]
            lhs = [t[i].astype(BF16) if last else jnp.concatenate([t[i].astype(BF16), pb[i]], axis=0)
                   for i in range(n)]
            tp = [_dot(lhs[i], pb[i]) for i in range(n)]
            yield
            t = [t[i] + tp[i][:C] for i in range(n)]
            if not last:
                p = [tp[i][C:] for i in range(n)]
        uw = [_dot(t[i].astype(BF16),
                   jnp.concatenate([v[i] * beta[i], kb[i] * eg[i]], axis=1).astype(BF16)) for i in range(n)]
        yield
        ws = [_dot(jnp.concatenate([uw[i][:, HEAD_DIM:], q[i] * eg[i]], axis=0).astype(BF16),
                   s[i].astype(BF16)) for i in range(n)]
        yield
        vnb = [(uw[i][:, :HEAD_DIM] - ws[i][:C]).astype(BF16) for i in range(n)]
        o = [ws[i][C:] + _dot(attn[i].astype(BF16), vnb[i]) for i in range(n)]
        for i, (bi, h) in enumerate(grp):
            g_last = gcb[i][C - 1:C, :]
            kd = (k[i] * jnp.exp(g_last - gcb[i])).astype(BF16)
            s_ref[bi * N_HEADS + h] = s[i] * jnp.exp(g_last) + _dot_tn(kd, vnb[i])
            o_ref[bi, :, _head_cols(h)] = o[i].astype(BF16)
        yield


def _mix_kernel(hq_ref, hk_ref, hb_ref, hv_ref, dq_ref, dk_ref, dv_ref, gb_ref, oa_ref, ob_ref,
                st_ref, s_ref, *, nb):
    @pl.when(pl.program_id(0) == 0)
    def _():
        st_ref[...] = jnp.zeros_like(st_ref)
        s_ref[...] = jnp.zeros_like(s_ref)

    hgrn = _hgrn_steps(hq_ref, hk_ref, hb_ref, hv_ref, oa_ref, st_ref, nb=nb, group=HGRN_GROUP)
    gdn = _gdn_steps(dq_ref, dk_ref, dv_ref, gb_ref, ob_ref, s_ref, nb=nb, group=GDN_GROUP)
    live = [gdn, hgrn]
    while live:
        live = [g for g in live if next(g, StopIteration) is not StopIteration]


def _mix(hq, hk, hb, hv, dq, dk, dv, gb, batch, seq, layer):
    blk = lambda t: (0, t, 0)
    spec = pl.BlockSpec((batch, CHUNK, HEADS_W), blk)
    r3 = lambda a: a.reshape(batch, seq, a.shape[-1])
    state = pltpu.VMEM((batch * N_HEADS, HEAD_DIM, HEAD_DIM), F32)
    out_sds = jax.ShapeDtypeStruct((batch, seq, HEADS_W), BF16)
    oa, ob = pl.pallas_call(
        functools.partial(_mix_kernel, nb=batch),
        grid=(seq // CHUNK,),
        in_specs=[spec] * 7 + [pl.BlockSpec((batch, CHUNK, LANES), blk)],
        out_specs=[spec, spec],
        out_shape=[out_sds, out_sds],
        scratch_shapes=[state, state],
        compiler_params=pltpu.CompilerParams(dimension_semantics=("arbitrary",),
                                             vmem_limit_bytes=VMEM_LIMIT),
        name=f"mix_l{layer}",
    )(r3(hq), r3(hk), r3(hb), r3(hv), r3(dq), r3(dk), r3(dv), r3(gb))
    return oa.reshape(batch * seq, HEADS_W), ob.reshape(batch * seq, HEADS_W)


def _head_norm_gate(o_ref, g_ref, w_ref, rows):
    outs = []
    for h in range(N_HEADS):
        cs = _head_cols(h)
        o = _rms(o_ref[rows, cs].astype(F32), w_ref[:, cs]) * _silu(g_ref[rows, cs].astype(F32))
        outs.append(o.astype(BF16))
    return jnp.concatenate(outs, axis=1)


def _post_kernel(x_ref, oa_ref, ga_ref, nwa_ref, ob_ref, gb_ref, nwb_ref, m_ref, woa_ref, wob_ref, wout_ref,
                 pmw_ref, pfw_ref, wg_ref, wu_ref, wd_ref, pow_ref, out_ref, *, ff_chunk, row_parts):
    d = x_ref.shape[1]
    d_ff = wg_ref.shape[1]
    part_rows = x_ref.shape[0] // row_parts

    def row_part(r0):
        rows = slice(r0, r0 + part_rows)
        ya = _dot(_head_norm_gate(oa_ref, ga_ref, nwa_ref, rows), woa_ref[...])
        yield
        yb = _dot(_head_norm_gate(ob_ref, gb_ref, nwb_ref, rows), wob_ref[...])
        yield
        merged = (_sigmoid(m_ref[rows, :d].astype(F32)) * ya + _sigmoid(m_ref[rows, d:].astype(F32)) * yb)
        mix = _dot(merged.astype(BF16), wout_ref[...])
        yield
        h = x_ref[rows, :] + _rms(mix, pmw_ref[...])
        v = _rms(h, pfw_ref[...]).astype(BF16)
        f = None
        for c0 in range(0, d_ff, ff_chunk):
            gate = _dot(v, wg_ref[:, c0:c0 + ff_chunk])
            up = _dot(v, wu_ref[:, c0:c0 + ff_chunk])
            yield
            part = _dot((_silu(gate) * up).astype(BF16), wd_ref[c0:c0 + ff_chunk, :])
            f = part if f is None else f + part
            yield
        out_ref[rows, :] = h + _rms(f, pow_ref[...])

    parts = [row_part(i * part_rows) for i in range(row_parts)]
    while parts:
        parts = [p for p in parts if next(p, StopIteration) is not StopIteration]


def _post(x, oa, ga, nwa, ob, gb, nwb, m, woa, wob, wout, pmw, pfw, wg, wu, wd, pow_, tm, layer):
    n, d = x.shape
    d_ff = wg.shape[1]
    ff_chunk = d_ff // 2 if (d_ff // 2) % LANES == 0 else d_ff
    row = lambda i: (i, 0)
    const = lambda i: (0, 0)
    resident = dict(pipeline_mode=pl.Buffered(1))

    def wspec(w):
        return pl.BlockSpec(w.shape, const, **resident)

    def tspec(a):
        return pl.BlockSpec((tm, a.shape[1]), row)

    def vspec(a):
        return pl.BlockSpec(a.shape, const)

    return pl.pallas_call(
        functools.partial(_post_kernel, ff_chunk=ff_chunk, row_parts=POST_ROW_PARTS),
        grid=(n // tm,),
        in_specs=[tspec(x), tspec(oa), tspec(ga), vspec(nwa), tspec(ob), tspec(gb), vspec(nwb), tspec(m),
                  wspec(woa), wspec(wob), wspec(wout), vspec(pmw), vspec(pfw),
                  wspec(wg), wspec(wu), wspec(wd), vspec(pow_)],
        out_specs=pl.BlockSpec((tm, d), row),
        out_shape=jax.ShapeDtypeStruct((n, d), F32),
        compiler_params=pltpu.CompilerParams(dimension_semantics=("arbitrary",),
                                             vmem_limit_bytes=VMEM_LIMIT),
        name=f"post_l{layer}",
    )(x, oa, ga, nwa, ob, gb, nwb, m, woa, wob, wout, pmw, pfw, wg, wu, wd, pow_)


def _pick_tile(n, want):
    t = min(n, want)
    while n % t:
        t //= 2
    return t


def kernel(x, hg_lower_bounds, pre_mix_w, w_in, hg_norm_w, conv_w, dn_a_log, dn_dt_bias, dn_norm_w,
           w_o_hg, w_o_dn, w_out, post_mix_w, pre_ffn_w, w_gate_up, w_down, post_ffn_w):
    batch, seq, d = x.shape
    depth = w_in.shape[0]
    n = batch * seq
    W = HEADS_W
    d_ff = w_down.shape[1]
    assert seq % CHUNK == 0 and w_in.shape[2] == 8 * W + 2 * N_HEADS + 2 * d
    tm = _pick_tile(seq, 512)
    assert tm % CHUNK == 0

    h = x.reshape(n, d)
    lb_raw = hg_lower_bounds.astype(F32)
    for l in range(depth):
        wl = w_in[l]
        w_main = jnp.concatenate([wl[:, :8 * W], wl[:, 8 * W + 2 * N_HEADS:]], axis=1).astype(BF16)
        w_ab = jnp.pad(wl[:, 8 * W:8 * W + 2 * N_HEADS], ((0, 0), (0, LANES - 2 * N_HEADS))).astype(BF16)
        par = jnp.pad(jnp.stack([dn_a_log[l], dn_dt_bias[l]]).astype(F32), ((0, 0), (0, LANES - N_HEADS)))
        hq, hk, hb, hi, hg, dq, dk, dv, dg, gb, m = _inproj(
            h, pre_mix_w[l].reshape(1, d), w_main, w_ab, lb_raw, conv_w[l].astype(F32), par, l, tm, seq)

        oa, ob = _mix(hq, hk, hb, hi, dq, dk, dv, gb, batch, seq, l)

        wgu = w_gate_up[l]
        h = _post(h, oa, hg, jnp.tile(hg_norm_w[l], N_HEADS).reshape(1, W),
                  ob, dg, jnp.tile(dn_norm_w[l], N_HEADS).reshape(1, W), m,
                  w_o_hg[l].astype(BF16), w_o_dn[l].astype(BF16), w_out[l].astype(BF16),
                  post_mix_w[l].reshape(1, d), pre_ffn_w[l].reshape(1, d),
                  wgu[:, :d_ff].astype(BF16), wgu[:, d_ff:].astype(BF16), w_down[l].astype(BF16),
                  post_ffn_w[l].reshape(1, d), tm, l)
    return h.reshape(batch, seq, d)
```

```python
import functools

import jax
import jax.numpy as jnp
from jax import lax
from jax.experimental import pallas as pl
from jax.experimental.pallas import tpu as pltpu

F32 = jnp.float32
BF16 = jnp.bfloat16

EPS = 1e-6
CHUNK = 64
HEAD_DIM = 128
LANES = 128
MXU_WIDTH = 256
N_HEADS = 4
HEADS_W = N_HEADS * HEAD_DIM
CONV_W = 4
SUBLANES = 8
HGRN_GROUP = 4
GDN_GROUP = 32
POST_ROW_PARTS = 2
VMEM_LIMIT = 56 * 1024 * 1024


def _dot(a, b):
    return jnp.dot(a, b, preferred_element_type=F32)


def _dot_nt(a, b):
    return lax.dot_general(a, b, (((1,), (1,)), ((), ())), preferred_element_type=F32)


def _dot_tn(a, b):
    return lax.dot_general(a, b, (((0,), (0,)), ((), ())), preferred_element_type=F32)


def _sigmoid(x):
    return 1.0 / (1.0 + jnp.exp(-x))


def _silu(x):
    return x * _sigmoid(x)


def _rms(x, w):
    return x * lax.rsqrt(jnp.mean(x * x, axis=-1, keepdims=True) + EPS) * w


def _tril_ones(n, dtype):
    r = lax.broadcasted_iota(jnp.int32, (n, n), 0)
    c = lax.broadcasted_iota(jnp.int32, (n, n), 1)
    return jnp.where(c <= r, 1.0, 0.0).astype(dtype)


def _chunk_cumsum(x, tril):
    hi = x.astype(BF16)
    lo = (x - hi.astype(F32)).astype(BF16)
    return _dot(tril, hi) + _dot(tril, lo)


def _group_row_bcast(x, group, row):
    n, w = x.shape
    x3 = x.reshape(n // group, group, w)
    return jnp.broadcast_to(x3[:, row:row + 1, :], x3.shape).reshape(n, w)


def _head_cols(h):
    return slice(h * HEAD_DIM, (h + 1) * HEAD_DIM)


def _inproj_kernel(x_ref, pw_ref, w_ref, wab_ref, lbraw_ref, cw_ref, par_ref,
                   hq_ref, hk_ref, hb_ref, hi_ref, hg_ref, dq_ref, dk_ref, dv_ref, dg_ref, gb_ref, m_ref,
                   halo_ref, *, layer, tm, tiles_per_seq):
    W = HEADS_W

    @pl.when(pl.program_id(0) % tiles_per_seq == 0)
    def _():
        halo_ref[...] = jnp.zeros_like(halo_ref)

    x = x_ref[...]
    u = _rms(x, pw_ref[...]).astype(BF16)
    tril = _tril_ones(CHUNK, BF16)
    row8 = lax.broadcasted_iota(jnp.int32, (SUBLANES, W), 0)

    def conv_silu(p, g):
        cols = slice(g * W, (g + 1) * W)
        halo = halo_ref[:, cols]
        halo_ref[:, cols] = p[tm - SUBLANES:tm, :]
        acc = p * cw_ref[CONV_W - 1:CONV_W, cols]
        for s in range(1, CONV_W):
            r = pltpu.roll(p, s, axis=0)
            head = jnp.where(row8 < s, pltpu.roll(halo, s, axis=0), r[0:SUBLANES, :])
            r = jnp.concatenate([head, r[SUBLANES:, :]], axis=0)
            acc = acc + r * cw_ref[CONV_W - 1 - s:CONV_W - s, cols]
        return _silu(acc)

    def l2norm_store(y, ref, scale):
        for h in range(N_HEADS):
            cs = _head_cols(h)
            yh = y[:, cs]
            ref[:, cs] = (yh * (lax.rsqrt(jnp.sum(yh * yh, axis=-1, keepdims=True) + EPS) * scale)).astype(BF16)

    def epi_dq(p):
        l2norm_store(conv_silu(p, 0), dq_ref, HEAD_DIM ** -0.5)

    def epi_dk(p):
        l2norm_store(conv_silu(p, 1), dk_ref, 1.0)

    def epi_dv(p):
        dv_ref[...] = conv_silu(p, 2).astype(BF16)

    def epi_cast(ref, c0=0):
        def epi(p):
            ref[:, c0:c0 + p.shape[1]] = p.astype(BF16)
        return epi

    def epi_f(f):
        depth = lbraw_ref.shape[0]
        rows = [lbraw_ref[r:r + 1, :] for r in range(depth)]
        mx = functools.reduce(jnp.maximum, rows)
        es = [jnp.exp(r - mx) for r in rows]
        tot = functools.reduce(lambda a, b: a + b, es)
        cum = functools.reduce(lambda a, b: a + b, es[:layer + 1])
        lb = (cum - es[0]) / tot
        e = jnp.exp(-jnp.abs(f))
        log_sig = jnp.minimum(f, 0.0) - jnp.log(1.0 + e)
        a = jnp.log(lb)
        c = jnp.log1p(-lb) + log_sig
        log_f = jnp.maximum(a, c) + jnp.log(1.0 + jnp.exp(-jnp.abs(a - c)))
        inv = 1.0 / (1.0 + e)
        hk_ref[...] = ((1.0 - lb) * jnp.where(f >= 0.0, e * inv, inv)).astype(BF16)
        for c0 in range(0, tm, CHUNK):
            hb_ref[c0:c0 + CHUNK, :] = _chunk_cumsum(log_f[c0:c0 + CHUNK, :], tril)

    def epi_ab(ab):
        z = ab + par_ref[1:2, :]
        softplus = jnp.maximum(z, 0.0) + jnp.log(1.0 + jnp.exp(-jnp.abs(z)))
        g_all = -jnp.exp(par_ref[0:1, :]) * softplus
        beta = _sigmoid(ab)
        lane = lax.broadcasted_iota(jnp.int32, (CHUNK, LANES), 1)
        for c0 in range(0, tm, CHUNK):
            gc = _chunk_cumsum(g_all[c0:c0 + CHUNK, :], tril)
            gb_ref[c0:c0 + CHUNK, :] = jnp.where(lane < N_HEADS, gc, beta[c0:c0 + CHUNK, :])

    groups = [(w_ref, 4 * W, epi_dq), (w_ref, 0, epi_cast(hq_ref)),
              (w_ref, 5 * W, epi_dk), (w_ref, 2 * W, epi_cast(hi_ref)),
              (w_ref, 6 * W, epi_dv), (w_ref, 3 * W, epi_cast(hg_ref)),
              (w_ref, W, epi_f), (w_ref, 7 * W, epi_cast(dg_ref))]
    groups += [(w_ref, 8 * W + c0, epi_cast(m_ref, c0)) for c0 in range(0, m_ref.shape[1], W)]
    groups.insert(9, (wab_ref, 0, epi_ab))
    pending = None
    for wref, c0, epi in groups:
        width = min(W, wref.shape[1] - c0)
        r = _dot(u, wref[:, c0:c0 + width])
        if pending is not None:
            pending[0](pending[1])
        pending = (epi, r)
    pending[0](pending[1])


def _inproj(x, pre_w, w_main, w_ab, lb_raw, conv_w, par, layer, tm, seq):
    n, d = x.shape
    wm = w_main.shape[1]
    m_w = wm - 8 * HEADS_W
    row = lambda i: (i, 0)
    const = lambda i: (0, 0)
    resident = dict(pipeline_mode=pl.Buffered(1))
    out_shapes = [
        jax.ShapeDtypeStruct((n, HEADS_W), BF16),
        jax.ShapeDtypeStruct((n, HEADS_W), BF16),
        jax.ShapeDtypeStruct((n, HEADS_W), F32),
        jax.ShapeDtypeStruct((n, HEADS_W), BF16),
        jax.ShapeDtypeStruct((n, HEADS_W), BF16),
        jax.ShapeDtypeStruct((n, HEADS_W), BF16),
        jax.ShapeDtypeStruct((n, HEADS_W), BF16),
        jax.ShapeDtypeStruct((n, HEADS_W), BF16),
        jax.ShapeDtypeStruct((n, HEADS_W), BF16),
        jax.ShapeDtypeStruct((n, LANES), F32),
        jax.ShapeDtypeStruct((n, m_w), BF16),
    ]
    out_specs = [pl.BlockSpec((tm, s.shape[1]), row) for s in out_shapes]
    return pl.pallas_call(
        functools.partial(_inproj_kernel, layer=layer, tm=tm, tiles_per_seq=seq // tm),
        grid=(n // tm,),
        in_specs=[
            pl.BlockSpec((tm, d), row),
            pl.BlockSpec((1, d), const),
            pl.BlockSpec((d, wm), const, **resident),
            pl.BlockSpec((d, LANES), const, **resident),
            pl.BlockSpec(lb_raw.shape, const),
            pl.BlockSpec(conv_w.shape, const),
            pl.BlockSpec(par.shape, const),
        ],
        out_specs=out_specs,
        out_shape=out_shapes,
        scratch_shapes=[pltpu.VMEM((SUBLANES, 3 * HEADS_W), F32)],
        compiler_params=pltpu.CompilerParams(dimension_semantics=("arbitrary",),
                                             vmem_limit_bytes=VMEM_LIMIT),
        name=f"inproj_l{layer}",
    )(x, pre_w, w_main, w_ab, lb_raw, conv_w, par)


def _row_bcast_load(ref, stream, group, row):
    bi, h = stream
    return jnp.concatenate(
        [jnp.broadcast_to(ref[bi, g0 + row:g0 + row + 1, _head_cols(h)], (group, HEAD_DIM))
         for g0 in range(0, CHUNK, group)], axis=0)


def _hgrn_steps(q_ref, k_ref, b_ref, v_ref, o_ref, st_ref, kf_ref, *, nb, group):
    C = CHUNK
    row = lax.broadcasted_iota(jnp.int32, (C, C), 0)
    col = lax.broadcasted_iota(jnp.int32, (C, C), 1)
    levels = (32, 16, 8)
    level_masks = [((row // m) % 2 == 1) & ((col // m) == (row // m) - 1) for m in levels]
    diag_masks = [(col == (row // SUBLANES) * SUBLANES + jj) & (jj <= row % SUBLANES)
                  for jj in range(SUBLANES)]
    ones = jnp.ones((HEAD_DIM, C), BF16)

    streams = [(bi, h) for bi in range(nb) for h in range(N_HEADS)]
    for g0 in range(0, len(streams), group):
        grp = streams[g0:g0 + group]
        n = len(grp)
        q = [q_ref[bi, :, _head_cols(h)].astype(F32) for bi, h in grp]
        k = [k_ref[bi, :, _head_cols(h)].astype(F32) for bi, h in grp]
        v = [v_ref[bi, :, _head_cols(h)] for bi, h in grp]
        b = [b_ref[bi, :, _head_cols(h)] for bi, h in grp]
        st = [st_ref[bi * N_HEADS + h] for bi, h in grp]
        for i, (bi, h) in enumerate(grp):
            kf_ref[bi, :, _head_cols(h)] = k[i]

        attn = [jnp.zeros((C, C), F32) for _ in range(n)]
        for m, mask in zip(levels, level_masks):
            for i in range(n):
                be = _row_bcast_load(b_ref, grp[i], 2 * m, m - 1)
                x = jnp.exp(-jnp.abs(b[i] - be))
                s = _dot_nt((q[i] * x).astype(BF16), (k[i] * x).astype(BF16))
                attn[i] = jnp.where(mask, s, attn[i])
        for i in range(n):
            ps = []
            for jj in range(SUBLANES):
                bj = _row_bcast_load(b_ref, grp[i], SUBLANES, jj)
                kj = _row_bcast_load(kf_ref, grp[i], SUBLANES, jj)
                ps.append((q[i] * kj * jnp.exp(b[i] - bj)).astype(BF16))
            rs = _dot(jnp.concatenate(ps, axis=0), ones)
            for jj in range(SUBLANES):
                attn[i] = jnp.where(diag_masks[jj], rs[jj * C:(jj + 1) * C, :], attn[i])

        o = [_dot_nt((q[i] * jnp.exp(b[i])).astype(BF16), st[i].astype(BF16)) for i in range(n)]
        o = [o[i] + _dot(attn[i].astype(BF16), v[i]) for i in range(n)]
        for i, (bi, h) in enumerate(grp):
            b_last = b_ref[bi, C - 1:C, _head_cols(h)]
            kd = (k[i] * jnp.exp(b_last - b[i])).astype(BF16)
            st_ref[bi * N_HEADS + h] = st[i] * jnp.exp(b_last) + _dot_tn(v[i], kd)
            o_ref[bi, :, _head_cols(h)] = o[i].astype(BF16)
        yield


def _gdn_steps(q_ref, k_ref, v_ref, gb_ref, o_ref, s_ref, *, nb, group):
    C = CHUNK
    row = lax.broadcasted_iota(jnp.int32, (C, C), 0)
    col = lax.broadcasted_iota(jnp.int32, (C, C), 1)
    causal = col <= row
    strict = col < row
    eye = jnp.where(col == row, 1.0, 0.0).astype(F32)

    streams = [(bi, h) for bi in range(nb) for h in range(N_HEADS)]
    for g0 in range(0, len(streams), group):
        grp = streams[g0:g0 + group]
        n = len(grp)
        q = [q_ref[bi, :, _head_cols(h)].astype(F32) for bi, h in grp]
        kbf = [k_ref[bi, :, _head_cols(h)] for bi, h in grp]
        k = [kbf[i].astype(F32) for i in range(n)]
        v = [v_ref[bi, :, _head_cols(h)].astype(F32) for bi, h in grp]
        s = [s_ref[bi * N_HEADS + h] for bi, h in grp]
        beta, gcb = [], []
        for bi, h in grp:
            gb = gb_ref[bi]
            gcb.append(jnp.broadcast_to(gb[:, h:h + 1], (C, HEAD_DIM)))
            beta.append(jnp.broadcast_to(gb[:, N_HEADS + h:N_HEADS + h + 1], (C, HEAD_DIM)))

        eg = [jnp.exp(gcb[i]) for i in range(n)]
        kb = [k[i] * beta[i] for i in range(n)]
        decay = []
        for i in range(n):
            g_row = jnp.transpose(gcb[i])[:C, :]
            decay.append(jnp.exp(jnp.minimum(gcb[i][:, :C] - g_row, 0.0)))
        qk = [_dot_nt(jnp.concatenate([q[i], kb[i]], axis=0).astype(BF16), kbf[i])
              for i in range(n)]
        yield
        attn = [jnp.where(causal, qk[i][:C] * decay[i], 0.0) for i in range(n)]
        a = [jnp.where(strict, qk[i][C:] * decay[i], 0.0) for i in range(n)]

        t = [eye - a[i] for i in range(n)]
        ab16 = [a[i].astype(BF16) for i in range(n)]
        p = [_dot(ab16[i], ab16[i]) for i in range(n)]
        yield
        for step in range(5):
            last = step == 4
            pb = [p[i].astype(BF16) for i in range(n)]
            lhs = [t[i].astype(BF16) if last else jnp.concatenate([t[i].astype(BF16), pb[i]], axis=0)
                   for i in range(n)]
            tp = [_dot(lhs[i], pb[i]) for i in range(n)]
            yield
            t = [t[i] + tp[i][:C] for i in range(n)]
            if not last:
                p = [tp[i][C:] for i in range(n)]
        uw = [_dot(t[i].astype(BF16),
                   jnp.concatenate([v[i] * beta[i], kb[i] * eg[i]], axis=1).astype(BF16)) for i in range(n)]
        yield
        ws = [_dot(jnp.concatenate([uw[i][:, HEAD_DIM:], q[i] * eg[i]], axis=0).astype(BF16),
                   s[i].astype(BF16)) for i in range(n)]
        yield
        vnb = [(uw[i][:, :HEAD_DIM] - ws[i][:C]).astype(BF16) for i in range(n)]
        o = [ws[i][C:] + _dot(attn[i].astype(BF16), vnb[i]) for i in range(n)]
        for i, (bi, h) in enumerate(grp):
            g_last = gcb[i][C - 1:C, :]
            kd = (k[i] * jnp.exp(g_last - gcb[i])).astype(BF16)
            s_ref[bi * N_HEADS + h] = s[i] * jnp.exp(g_last) + _dot_tn(kd, vnb[i])
            o_ref[bi, :, _head_cols(h)] = o[i].astype(BF16)
        yield


def _mix_kernel(hq_ref, hk_ref, hb_ref, hv_ref, dq_ref, dk_ref, dv_ref, gb_ref, oa_ref, ob_ref,
                st_ref, s_ref, kf_ref, *, nb):
    @pl.when(pl.program_id(0) == 0)
    def _():
        st_ref[...] = jnp.zeros_like(st_ref)
        s_ref[...] = jnp.zeros_like(s_ref)

    hgrn = _hgrn_steps(hq_ref, hk_ref, hb_ref, hv_ref, oa_ref, st_ref, kf_ref, nb=nb, group=HGRN_GROUP)
    gdn = _gdn_steps(dq_ref, dk_ref, dv_ref, gb_ref, ob_ref, s_ref, nb=nb, group=GDN_GROUP)
    live = [gdn, hgrn]
    while live:
        live = [g for g in live if next(g, StopIteration) is not StopIteration]


def _mix(hq, hk, hb, hv, dq, dk, dv, gb, batch, seq, layer):
    blk = lambda t: (0, t, 0)
    spec = pl.BlockSpec((batch, CHUNK, HEADS_W), blk)
    r3 = lambda a: a.reshape(batch, seq, a.shape[-1])
    state = pltpu.VMEM((batch * N_HEADS, HEAD_DIM, HEAD_DIM), F32)
    out_sds = jax.ShapeDtypeStruct((batch, seq, HEADS_W), BF16)
    oa, ob = pl.pallas_call(
        functools.partial(_mix_kernel, nb=batch),
        grid=(seq // CHUNK,),
        in_specs=[spec] * 7 + [pl.BlockSpec((batch, CHUNK, LANES), blk)],
        out_specs=[spec, spec],
        out_shape=[out_sds, out_sds],
        scratch_shapes=[state, state, pltpu.VMEM((batch, CHUNK, HEADS_W), F32)],
        compiler_params=pltpu.CompilerParams(dimension_semantics=("arbitrary",),
                                             vmem_limit_bytes=VMEM_LIMIT),
        name=f"mix_l{layer}",
    )(r3(hq), r3(hk), r3(hb), r3(hv), r3(dq), r3(dk), r3(dv), r3(gb))
    return oa.reshape(batch * seq, HEADS_W), ob.reshape(batch * seq, HEADS_W)


def _head_norm_gate(o_ref, g_ref, w_ref, rows):
    outs = []
    for h in range(N_HEADS):
        cs = _head_cols(h)
        o = _rms(o_ref[rows, cs].astype(F32), w_ref[:, cs]) * _silu(g_ref[rows, cs].astype(F32))
        outs.append(o.astype(BF16))
    return jnp.concatenate(outs, axis=1)


def _post_kernel(x_ref, oa_ref, ga_ref, nwa_ref, ob_ref, gb_ref, nwb_ref, m_ref, woa_ref, wob_ref, wout_ref,
                 pmw_ref, pfw_ref, wg_ref, wu_ref, wd_ref, pow_ref, out_ref, *, ff_bounds, row_parts):
    d = x_ref.shape[1]
    d_ff = wg_ref.shape[1]
    part_rows = x_ref.shape[0] // row_parts

    def row_part(r0):
        rows = slice(r0, r0 + part_rows)
        ya = _dot(_head_norm_gate(oa_ref, ga_ref, nwa_ref, rows), woa_ref[...])
        yield
        yb = _dot(_head_norm_gate(ob_ref, gb_ref, nwb_ref, rows), wob_ref[...])
        yield
        merged = (_sigmoid(m_ref[rows, :d].astype(F32)) * ya + _sigmoid(m_ref[rows, d:].astype(F32)) * yb)
        mix = _dot(merged.astype(BF16), wout_ref[...])
        yield
        h = x_ref[rows, :] + _rms(mix, pmw_ref[...])
        v = _rms(h, pfw_ref[...]).astype(BF16)
        f = None
        for c0, c1 in zip(ff_bounds[:-1], ff_bounds[1:]):
            gate = _dot(v, wg_ref[:, c0:c1])
            up = _dot(v, wu_ref[:, c0:c1])
            yield
            part = _dot((_silu(gate) * up).astype(BF16), wd_ref[c0:c1, :])
            f = part if f is None else f + part
            yield
        out_ref[rows, :] = h + _rms(f, pow_ref[...])

    parts = [row_part(i * part_rows) for i in range(row_parts)]
    while parts:
        parts = [p for p in parts if next(p, StopIteration) is not StopIteration]


def _post(x, oa, ga, nwa, ob, gb, nwb, m, woa, wob, wout, pmw, pfw, wg, wu, wd, pow_, tm, layer):
    n, d = x.shape
    d_ff = wg.shape[1]
    ff_split = -(-(d_ff // MXU_WIDTH) // 2) * MXU_WIDTH
    ff_bounds = (0, ff_split, d_ff) if 0 < ff_split < d_ff else (0, d_ff)
    row = lambda i: (i, 0)
    const = lambda i: (0, 0)
    resident = dict(pipeline_mode=pl.Buffered(1))

    def wspec(w):
        return pl.BlockSpec(w.shape, const, **resident)

    def tspec(a):
        return pl.BlockSpec((tm, a.shape[1]), row)

    def vspec(a):
        return pl.BlockSpec(a.shape, const)

    return pl.pallas_call(
        functools.partial(_post_kernel, ff_bounds=ff_bounds, row_parts=POST_ROW_PARTS),
        grid=(n // tm,),
        in_specs=[tspec(x), tspec(oa), tspec(ga), vspec(nwa), tspec(ob), tspec(gb), vspec(nwb), tspec(m),
                  wspec(woa), wspec(wob), wspec(wout), vspec(pmw), vspec(pfw),
                  wspec(wg), wspec(wu), wspec(wd), vspec(pow_)],
        out_specs=pl.BlockSpec((tm, d), row),
        out_shape=jax.ShapeDtypeStruct((n, d), F32),
        compiler_params=pltpu.CompilerParams(dimension_semantics=("arbitrary",),
                                             vmem_limit_bytes=VMEM_LIMIT),
        name=f"post_l{layer}",
    )(x, oa, ga, nwa, ob, gb, nwb, m, woa, wob, wout, pmw, pfw, wg, wu, wd, pow_)


def _pick_tile(n, want):
    t = min(n, want)
    while n % t:
        t //= 2
    return t


def kernel(x, hg_lower_bounds, pre_mix_w, w_in, hg_norm_w, conv_w, dn_a_log, dn_dt_bias, dn_norm_w,
           w_o_hg, w_o_dn, w_out, post_mix_w, pre_ffn_w, w_gate_up, w_down, post_ffn_w):
    batch, seq, d = x.shape
    depth = w_in.shape[0]
    n = batch * seq
    W = HEADS_W
    d_ff = w_down.shape[1]
    assert seq % CHUNK == 0 and w_in.shape[2] == 8 * W + 2 * N_HEADS + 2 * d
    tm = _pick_tile(seq, 512)
    assert tm % CHUNK == 0

    h = x.reshape(n, d)
    lb_raw = hg_lower_bounds.astype(F32)
    for l in range(depth):
        wl = w_in[l]
        w_main = jnp.concatenate([wl[:, :8 * W], wl[:, 8 * W + 2 * N_HEADS:]], axis=1).astype(BF16)
        w_ab = jnp.pad(wl[:, 8 * W:8 * W + 2 * N_HEADS], ((0, 0), (0, LANES - 2 * N_HEADS))).astype(BF16)
        par = jnp.pad(jnp.stack([dn_a_log[l], dn_dt_bias[l]]).astype(F32), ((0, 0), (0, LANES - N_HEADS)))
        hq, hk, hb, hi, hg, dq, dk, dv, dg, gb, m = _inproj(
            h, pre_mix_w[l].reshape(1, d), w_main, w_ab, lb_raw, conv_w[l].astype(F32), par, l, tm, seq)

        oa, ob = _mix(hq, hk, hb, hi, dq, dk, dv, gb, batch, seq, l)

        wgu = w_gate_up[l]
        h = _post(h, oa, hg, jnp.tile(hg_norm_w[l], N_HEADS).reshape(1, W),
                  ob, dg, jnp.tile(dn_norm_w[l], N_HEADS).reshape(1, W), m,
                  w_o_hg[l].astype(BF16), w_o_dn[l].astype(BF16), w_out[l].astype(BF16),
                  post_mix_w[l].reshape(1, d), pre_ffn_w[l].reshape(1, d),
                  wgu[:, :d_ff].astype(BF16), wgu[:, d_ff:].astype(BF16), w_down[l].astype(BF16),
                  post_ffn_w[l].reshape(1, d), tm, l)
    return h.reshape(batch, seq, d)
```

```python
import functools

import jax
import jax.numpy as jnp
from jax import lax
from jax.experimental import pallas as pl
from jax.experimental.pallas import tpu as pltpu

F32 = jnp.float32
BF16 = jnp.bfloat16

EPS = 1e-6
CHUNK = 64
HEAD_DIM = 128
LANES = 128
MXU_WIDTH = 256
N_HEADS = 4
HEADS_W = N_HEADS * HEAD_DIM
CONV_W = 4
SUBLANES = 8
HGRN_GROUP = 4
GDN_GROUP = 32
POST_ROW_PARTS = 2
VMEM_LIMIT = 56 * 1024 * 1024


def _dot(a, b):
    return jnp.dot(a, b, preferred_element_type=F32)


def _dot_nt(a, b):
    return lax.dot_general(a, b, (((1,), (1,)), ((), ())), preferred_element_type=F32)


def _dot_tn(a, b):
    return lax.dot_general(a, b, (((0,), (0,)), ((), ())), preferred_element_type=F32)


def _sigmoid(x):
    return 1.0 / (1.0 + jnp.exp(-x))


def _silu(x):
    return x * _sigmoid(x)


def _rms(x, w):
    return x * lax.rsqrt(jnp.mean(x * x, axis=-1, keepdims=True) + EPS) * w


def _tril_ones(n, dtype):
    r = lax.broadcasted_iota(jnp.int32, (n, n), 0)
    c = lax.broadcasted_iota(jnp.int32, (n, n), 1)
    return jnp.where(c <= r, 1.0, 0.0).astype(dtype)


def _chunk_cumsum(x, tril):
    hi = x.astype(BF16)
    lo = (x - hi.astype(F32)).astype(BF16)
    return _dot(tril, hi) + _dot(tril, lo)


def _head_cols(h):
    return slice(h * HEAD_DIM, (h + 1) * HEAD_DIM)


def _inproj_kernel(x_ref, pw_ref, w_ref, wab_ref, lbraw_ref, cw_ref, par_ref,
                   hq_ref, hk_ref, hb_ref, hi_ref, hg_ref, dq_ref, dk_ref, dv_ref, dg_ref, gb_ref, m_ref,
                   halo_ref, *, layer, tm, tiles_per_seq):
    W = HEADS_W

    @pl.when(pl.program_id(0) % tiles_per_seq == 0)
    def _():
        halo_ref[...] = jnp.zeros_like(halo_ref)

    x = x_ref[...]
    u = _rms(x, pw_ref[...]).astype(BF16)
    tril = _tril_ones(CHUNK, BF16)
    row8 = lax.broadcasted_iota(jnp.int32, (SUBLANES, W), 0)

    def conv_silu(p, g):
        cols = slice(g * W, (g + 1) * W)
        halo = halo_ref[:, cols]
        halo_ref[:, cols] = p[tm - SUBLANES:tm, :]
        acc = p * cw_ref[CONV_W - 1:CONV_W, cols]
        for s in range(1, CONV_W):
            r = pltpu.roll(p, s, axis=0)
            head = jnp.where(row8 < s, pltpu.roll(halo, s, axis=0), r[0:SUBLANES, :])
            r = jnp.concatenate([head, r[SUBLANES:, :]], axis=0)
            acc = acc + r * cw_ref[CONV_W - 1 - s:CONV_W - s, cols]
        return _silu(acc)

    def l2norm_store(y, ref, scale):
        for h in range(N_HEADS):
            cs = _head_cols(h)
            yh = y[:, cs]
            ref[:, cs] = (yh * (lax.rsqrt(jnp.sum(yh * yh, axis=-1, keepdims=True) + EPS) * scale)).astype(BF16)

    def epi_dq(p):
        l2norm_store(conv_silu(p, 0), dq_ref, HEAD_DIM ** -0.5)

    def epi_dk(p):
        l2norm_store(conv_silu(p, 1), dk_ref, 1.0)

    def epi_dv(p):
        dv_ref[...] = conv_silu(p, 2).astype(BF16)

    def epi_cast(ref, c0=0):
        def epi(p):
            ref[:, c0:c0 + p.shape[1]] = p.astype(BF16)
        return epi

    def epi_f(f):
        depth = lbraw_ref.shape[0]
        rows = [lbraw_ref[r:r + 1, :] for r in range(depth)]
        mx = functools.reduce(jnp.maximum, rows)
        es = [jnp.exp(r - mx) for r in rows]
        tot = functools.reduce(lambda a, b: a + b, es)
        cum = functools.reduce(lambda a, b: a + b, es[:layer + 1])
        lb = (cum - es[0]) / tot
        e = jnp.exp(-jnp.abs(f))
        log_sig = jnp.minimum(f, 0.0) - jnp.log(1.0 + e)
        a = jnp.log(lb)
        c = jnp.log1p(-lb) + log_sig
        log_f = jnp.maximum(a, c) + jnp.log(1.0 + jnp.exp(-jnp.abs(a - c)))
        inv = 1.0 / (1.0 + e)
        hk_ref[...] = ((1.0 - lb) * jnp.where(f >= 0.0, e * inv, inv)).astype(BF16)
        for c0 in range(0, tm, CHUNK):
            hb_ref[c0:c0 + CHUNK, :] = _chunk_cumsum(log_f[c0:c0 + CHUNK, :], tril)

    def epi_ab(ab):
        z = ab + par_ref[1:2, :]
        softplus = jnp.maximum(z, 0.0) + jnp.log(1.0 + jnp.exp(-jnp.abs(z)))
        g_all = -jnp.exp(par_ref[0:1, :]) * softplus
        beta = _sigmoid(ab)
        lane = lax.broadcasted_iota(jnp.int32, (CHUNK, LANES), 1)
        for c0 in range(0, tm, CHUNK):
            gc = _chunk_cumsum(g_all[c0:c0 + CHUNK, :], tril)
            gb_ref[c0:c0 + CHUNK, :] = jnp.where(lane < N_HEADS, gc, beta[c0:c0 + CHUNK, :])

    groups = [(w_ref, 4 * W, epi_dq), (w_ref, 0, epi_cast(hq_ref)),
              (w_ref, 5 * W, epi_dk), (w_ref, 2 * W, epi_cast(hi_ref)),
              (w_ref, 6 * W, epi_dv), (w_ref, 3 * W, epi_cast(hg_ref)),
              (w_ref, W, epi_f), (w_ref, 7 * W, epi_cast(dg_ref))]
    groups += [(w_ref, 8 * W + c0, epi_cast(m_ref, c0)) for c0 in range(0, m_ref.shape[1], W)]
    groups.insert(9, (wab_ref, 0, epi_ab))
    pending = None
    for wref, c0, epi in groups:
        width = min(W, wref.shape[1] - c0)
        r = _dot(u, wref[:, c0:c0 + width])
        if pending is not None:
            pending[0](pending[1])
        pending = (epi, r)
    pending[0](pending[1])


def _inproj(x, pre_w, w_main, w_ab, lb_raw, conv_w, par, layer, tm, seq):
    n, d = x.shape
    wm = w_main.shape[1]
    m_w = wm - 8 * HEADS_W
    row = lambda i: (i, 0)
    const = lambda i: (0, 0)
    resident = dict(pipeline_mode=pl.Buffered(1))
    out_shapes = [
        jax.ShapeDtypeStruct((n, HEADS_W), BF16),
        jax.ShapeDtypeStruct((n, HEADS_W), BF16),
        jax.ShapeDtypeStruct((n, HEADS_W), F32),
        jax.ShapeDtypeStruct((n, HEADS_W), BF16),
        jax.ShapeDtypeStruct((n, HEADS_W), BF16),
        jax.ShapeDtypeStruct((n, HEADS_W), BF16),
        jax.ShapeDtypeStruct((n, HEADS_W), BF16),
        jax.ShapeDtypeStruct((n, HEADS_W), BF16),
        jax.ShapeDtypeStruct((n, HEADS_W), BF16),
        jax.ShapeDtypeStruct((n, LANES), F32),
        jax.ShapeDtypeStruct((n, m_w), BF16),
    ]
    out_specs = [pl.BlockSpec((tm, s.shape[1]), row) for s in out_shapes]
    return pl.pallas_call(
        functools.partial(_inproj_kernel, layer=layer, tm=tm, tiles_per_seq=seq // tm),
        grid=(n // tm,),
        in_specs=[
            pl.BlockSpec((tm, d), row),
            pl.BlockSpec((1, d), const),
            pl.BlockSpec((d, wm), const, **resident),
            pl.BlockSpec((d, LANES), const, **resident),
            pl.BlockSpec(lb_raw.shape, const),
            pl.BlockSpec(conv_w.shape, const),
            pl.BlockSpec(par.shape, const),
        ],
        out_specs=out_specs,
        out_shape=out_shapes,
        scratch_shapes=[pltpu.VMEM((SUBLANES, 3 * HEADS_W), F32)],
        compiler_params=pltpu.CompilerParams(dimension_semantics=("arbitrary",),
                                             vmem_limit_bytes=VMEM_LIMIT),
        name=f"inproj_l{layer}",
    )(x, pre_w, w_main, w_ab, lb_raw, conv_w, par)


def _row_bcast_load(ref, stream, group, row):
    bi, h = stream
    return jnp.concatenate(
        [jnp.broadcast_to(ref[bi, g0 + row:g0 + row + 1, _head_cols(h)], (group, HEAD_DIM))
         for g0 in range(0, CHUNK, group)], axis=0)


def _hgrn_steps(q_ref, k_ref, b_ref, v_ref, o_ref, st_ref, kf_ref, *, nb, group):
    C = CHUNK
    row = lax.broadcasted_iota(jnp.int32, (C, C), 0)
    col = lax.broadcasted_iota(jnp.int32, (C, C), 1)
    levels = (32, 16, 8)
    level_masks = [((row // m) % 2 == 1) & ((col // m) == (row // m) - 1) for m in levels]
    diag_masks = [(col == (row // SUBLANES) * SUBLANES + jj) & (jj <= row % SUBLANES)
                  for jj in range(SUBLANES)]
    ones = jnp.ones((HEAD_DIM, C), BF16)

    streams = [(bi, h) for bi in range(nb) for h in range(N_HEADS)]
    for g0 in range(0, len(streams), group):
        grp = streams[g0:g0 + group]
        n = len(grp)
        q = [q_ref[bi, :, _head_cols(h)].astype(F32) for bi, h in grp]
        k = [k_ref[bi, :, _head_cols(h)].astype(F32) for bi, h in grp]
        v = [v_ref[bi, :, _head_cols(h)] for bi, h in grp]
        b = [b_ref[bi, :, _head_cols(h)] for bi, h in grp]
        st = [st_ref[bi * N_HEADS + h] for bi, h in grp]
        for i, (bi, h) in enumerate(grp):
            kf_ref[bi, :, _head_cols(h)] = k[i]

        attn = [jnp.zeros((C, C), F32) for _ in range(n)]
        for m, mask in zip(levels, level_masks):
            for i in range(n):
                be = _row_bcast_load(b_ref, grp[i], 2 * m, m - 1)
                x = jnp.exp(-jnp.abs(b[i] - be))
                s = _dot_nt((q[i] * x).astype(BF16), (k[i] * x).astype(BF16))
                attn[i] = jnp.where(mask, s, attn[i])
        for i in range(n):
            ps = []
            for jj in range(SUBLANES):
                bj = _row_bcast_load(b_ref, grp[i], SUBLANES, jj)
                kj = _row_bcast_load(kf_ref, grp[i], SUBLANES, jj)
                ps.append((q[i] * kj * jnp.exp(b[i] - bj)).astype(BF16))
            rs = _dot(jnp.concatenate(ps, axis=0), ones)
            for jj in range(SUBLANES):
                attn[i] = jnp.where(diag_masks[jj], rs[jj * C:(jj + 1) * C, :], attn[i])

        o = [_dot_nt((q[i] * jnp.exp(b[i])).astype(BF16), st[i].astype(BF16)) for i in range(n)]
        o = [o[i] + _dot(attn[i].astype(BF16), v[i]) for i in range(n)]
        for i, (bi, h) in enumerate(grp):
            b_last = b_ref[bi, C - 1:C, _head_cols(h)]
            kd = (k[i] * jnp.exp(b_last - b[i])).astype(BF16)
            st_ref[bi * N_HEADS + h] = st[i] * jnp.exp(b_last) + _dot_tn(v[i], kd)
            o_ref[bi, :, _head_cols(h)] = o[i].astype(BF16)
        yield


def _gdn_steps(q_ref, k_ref, v_ref, gb_ref, o_ref, s_ref, *, nb, group):
    C = CHUNK
    row = lax.broadcasted_iota(jnp.int32, (C, C), 0)
    col = lax.broadcasted_iota(jnp.int32, (C, C), 1)
    causal = col <= row
    strict = col < row
    eye = jnp.where(col == row, 1.0, 0.0).astype(F32)

    streams = [(bi, h) for bi in range(nb) for h in range(N_HEADS)]
    for g0 in range(0, len(streams), group):
        grp = streams[g0:g0 + group]
        n = len(grp)
        q = [q_ref[bi, :, _head_cols(h)].astype(F32) for bi, h in grp]
        kbf = [k_ref[bi, :, _head_cols(h)] for bi, h in grp]
        k = [kbf[i].astype(F32) for i in range(n)]
        v = [v_ref[bi, :, _head_cols(h)].astype(F32) for bi, h in grp]
        s = [s_ref[bi * N_HEADS + h] for bi, h in grp]
        beta, gcb = [], []
        for bi, h in grp:
            gb = gb_ref[bi]
            gcb.append(jnp.broadcast_to(gb[:, h:h + 1], (C, HEAD_DIM)))
            beta.append(jnp.broadcast_to(gb[:, N_HEADS + h:N_HEADS + h + 1], (C, HEAD_DIM)))

        eg = [jnp.exp(gcb[i]) for i in range(n)]
        kb = [k[i] * beta[i] for i in range(n)]
        decay = []
        for i in range(n):
            g_row = jnp.transpose(gcb[i])[:C, :]
            decay.append(jnp.exp(jnp.minimum(gcb[i][:, :C] - g_row, 0.0)))
        qk = [_dot_nt(jnp.concatenate([q[i], kb[i]], axis=0).astype(BF16), kbf[i])
              for i in range(n)]
        yield
        attn = [jnp.where(causal, qk[i][:C] * decay[i], 0.0) for i in range(n)]
        a = [jnp.where(strict, qk[i][C:] * decay[i], 0.0) for i in range(n)]

        t = [eye - a[i] for i in range(n)]
        ab16 = [a[i].astype(BF16) for i in range(n)]
        p = [_dot(ab16[i], ab16[i]) for i in range(n)]
        yield
        for step in range(5):
            last = step == 4
            pb = [p[i].astype(BF16) for i in range(n)]
            lhs = [t[i].astype(BF16) if last else jnp.concatenate([t[i].astype(BF16), pb[i]], axis=0)
                   for i in range(n)]
            tp = [_dot(lhs[i], pb[i]) for i in range(n)]
            yield
            t = [t[i] + tp[i][:C] for i in range(n)]
            if not last:
                p = [tp[i][C:] for i in range(n)]
        uw = [_dot(t[i].astype(BF16),
                   jnp.concatenate([v[i] * beta[i], kb[i] * eg[i]], axis=1).astype(BF16)) for i in range(n)]
        yield
        ws = [_dot(jnp.concatenate([uw[i][:, HEAD_DIM:], q[i] * eg[i]], axis=0).astype(BF16),
                   s[i].astype(BF16)) for i in range(n)]
        yield
        vnb = [(uw[i][:, :HEAD_DIM] - ws[i][:C]).astype(BF16) for i in range(n)]
        o = [ws[i][C:] + _dot(attn[i].astype(BF16), vnb[i]) for i in range(n)]
        for i, (bi, h) in enumerate(grp):
            g_last = gcb[i][C - 1:C, :]
            kd = (k[i] * jnp.exp(g_last - gcb[i])).astype(BF16)
            s_ref[bi * N_HEADS + h] = s[i] * jnp.exp(g_last) + _dot_tn(kd, vnb[i])
            o_ref[bi, :, _head_cols(h)] = o[i].astype(BF16)
        yield


def _mix_kernel(hq_ref, hk_ref, hb_ref, hv_ref, dq_ref, dk_ref, dv_ref, gb_ref, oa_ref, ob_ref,
                st_ref, s_ref, kf_ref, *, nb):
    @pl.when(pl.program_id(0) == 0)
    def _():
        st_ref[...] = jnp.zeros_like(st_ref)
        s_ref[...] = jnp.zeros_like(s_ref)

    hgrn = _hgrn_steps(hq_ref, hk_ref, hb_ref, hv_ref, oa_ref, st_ref, kf_ref, nb=nb, group=HGRN_GROUP)
    gdn = _gdn_steps(dq_ref, dk_ref, dv_ref, gb_ref, ob_ref, s_ref, nb=nb, group=GDN_GROUP)
    live = [gdn, hgrn]
    while live:
        live = [g for g in live if next(g, StopIteration) is not StopIteration]


def _mix(hq, hk, hb, hv, dq, dk, dv, gb, batch, seq, layer):
    blk = lambda t: (0, t, 0)
    spec = pl.BlockSpec((batch, CHUNK, HEADS_W), blk)
    r3 = lambda a: a.reshape(batch, seq, a.shape[-1])
    state = pltpu.VMEM((batch * N_HEADS, HEAD_DIM, HEAD_DIM), F32)
    out_sds = jax.ShapeDtypeStruct((batch, seq, HEADS_W), BF16)
    oa, ob = pl.pallas_call(
        functools.partial(_mix_kernel, nb=batch),
        grid=(seq // CHUNK,),
        in_specs=[spec] * 7 + [pl.BlockSpec((batch, CHUNK, LANES), blk)],
        out_specs=[spec, spec],
        out_shape=[out_sds, out_sds],
        scratch_shapes=[state, state, pltpu.VMEM((batch, CHUNK, HEADS_W), F32)],
        compiler_params=pltpu.CompilerParams(dimension_semantics=("arbitrary",),
                                             vmem_limit_bytes=VMEM_LIMIT),
        name=f"mix_l{layer}",
    )(r3(hq), r3(hk), r3(hb), r3(hv), r3(dq), r3(dk), r3(dv), r3(gb))
    return oa.reshape(batch * seq, HEADS_W), ob.reshape(batch * seq, HEADS_W)


def _head_norm_gate(o_ref, g_ref, w_ref, rows):
    outs = []
    for h in range(N_HEADS):
        cs = _head_cols(h)
        o = _rms(o_ref[rows, cs].astype(F32), w_ref[:, cs]) * _silu(g_ref[rows, cs].astype(F32))
        outs.append(o.astype(BF16))
    return jnp.concatenate(outs, axis=1)


def _post_kernel(x_ref, oa_ref, ga_ref, nwa_ref, ob_ref, gb_ref, nwb_ref, m_ref, woa_ref, wob_ref, wout_ref,
                 pmw_ref, pfw_ref, wg_ref, wu_ref, wd_ref, pow_ref, out_ref, *, ff_bounds, row_parts):
    d = x_ref.shape[1]
    d_ff = wg_ref.shape[1]
    part_rows = x_ref.shape[0] // row_parts

    def row_part(r0):
        rows = slice(r0, r0 + part_rows)
        ya = _dot(_head_norm_gate(oa_ref, ga_ref, nwa_ref, rows), woa_ref[...])
        yield
        yb = _dot(_head_norm_gate(ob_ref, gb_ref, nwb_ref, rows), wob_ref[...])
        yield
        merged = (_sigmoid(m_ref[rows, :d].astype(F32)) * ya + _sigmoid(m_ref[rows, d:].astype(F32)) * yb)
        mix = _dot(merged.astype(BF16), wout_ref[...])
        yield
        h = x_ref[rows, :] + _rms(mix, pmw_ref[...])
        v = _rms(h, pfw_ref[...]).astype(BF16)
        f = None
        for c0, c1 in zip(ff_bounds[:-1], ff_bounds[1:]):
            gate = _dot(v, wg_ref[:, c0:c1])
            up = _dot(v, wu_ref[:, c0:c1])
            yield
            part = _dot((_silu(gate) * up).astype(BF16), wd_ref[c0:c1, :])
            f = part if f is None else f + part
            yield
        out_ref[rows, :] = h + _rms(f, pow_ref[...])

    parts = [row_part(i * part_rows) for i in range(row_parts)]
    while parts:
        parts = [p for p in parts if next(p, StopIteration) is not StopIteration]


def _post(x, oa, ga, nwa, ob, gb, nwb, m, woa, wob, wout, pmw, pfw, wg, wu, wd, pow_, tm, layer):
    n, d = x.shape
    d_ff = wg.shape[1]
    ff_split = -(-(d_ff // MXU_WIDTH) // 2) * MXU_WIDTH
    ff_bounds = (0, ff_split, d_ff) if 0 < ff_split < d_ff else (0, d_ff)
    row = lambda i: (i, 0)
    const = lambda i: (0, 0)
    resident = dict(pipeline_mode=pl.Buffered(1))

    def wspec(w):
        return pl.BlockSpec(w.shape, const, **resident)

    def tspec(a):
        return pl.BlockSpec((tm, a.shape[1]), row)

    def vspec(a):
        return pl.BlockSpec(a.shape, const)

    return pl.pallas_call(
        functools.partial(_post_kernel, ff_bounds=ff_bounds, row_parts=POST_ROW_PARTS),
        grid=(n // tm,),
        in_specs=[tspec(x), tspec(oa), tspec(ga), vspec(nwa), tspec(ob), tspec(gb), vspec(nwb), tspec(m),
                  wspec(woa), wspec(wob), wspec(wout), vspec(pmw), vspec(pfw),
                  wspec(wg), wspec(wu), wspec(wd), vspec(pow_)],
        out_specs=pl.BlockSpec((tm, d), row),
        out_shape=jax.ShapeDtypeStruct((n, d), F32),
        compiler_params=pltpu.CompilerParams(dimension_semantics=("arbitrary",),
                                             vmem_limit_bytes=VMEM_LIMIT),
        name=f"post_l{layer}",
    )(x, oa, ga, nwa, ob, gb, nwb, m, woa, wob, wout, pmw, pfw, wg, wu, wd, pow_)


def _pick_tile(n, want):
    t = min(n, want)
    while n % t:
        t //= 2
    return t


def kernel(x, hg_lower_bounds, pre_mix_w, w_in, hg_norm_w, conv_w, dn_a_log, dn_dt_bias, dn_norm_w,
           w_o_hg, w_o_dn, w_out, post_mix_w, pre_ffn_w, w_gate_up, w_down, post_ffn_w):
    batch, seq, d = x.shape
    depth = w_in.shape[0]
    n = batch * seq
    W = HEADS_W
    d_ff = w_down.shape[1]
    assert seq % CHUNK == 0 and w_in.shape[2] == 8 * W + 2 * N_HEADS + 2 * d
    tm = _pick_tile(seq, 512)
    assert tm % CHUNK == 0

    h = x.reshape(n, d)
    lb_raw = hg_lower_bounds.astype(F32)
    for l in range(depth):
        wl = w_in[l]
        w_main = jnp.concatenate([wl[:, :8 * W], wl[:, 8 * W + 2 * N_HEADS:]], axis=1).astype(BF16)
        w_ab = jnp.pad(wl[:, 8 * W:8 * W + 2 * N_HEADS], ((0, 0), (0, LANES - 2 * N_HEADS))).astype(BF16)
        par = jnp.pad(jnp.stack([dn_a_log[l], dn_dt_bias[l]]).astype(F32), ((0, 0), (0, LANES - N_HEADS)))
        hq, hk, hb, hi, hg, dq, dk, dv, dg, gb, m = _inproj(
            h, pre_mix_w[l].reshape(1, d), w_main, w_ab, lb_raw, conv_w[l].astype(F32), par, l, tm, seq)

        oa, ob = _mix(hq, hk, hb, hi, dq, dk, dv, gb, batch, seq, l)

        wgu = w_gate_up[l]
        h = _post(h, oa, hg, jnp.tile(hg_norm_w[l], N_HEADS).reshape(1, W),
                  ob, dg, jnp.tile(dn_norm_w[l], N_HEADS).reshape(1, W), m,
                  w_o_hg[l].astype(BF16), w_o_dn[l].astype(BF16), w_out[l].astype(BF16),
                  post_mix_w[l].reshape(1, d), pre_ffn_w[l].reshape(1, d),
                  wgu[:, :d_ff].astype(BF16), wgu[:, d_ff:].astype(BF16), w_down[l].astype(BF16),
                  post_ffn_w[l].reshape(1, d), tm, l)
    return h.reshape(batch, seq, d)
```
